```python
import jax, jax.numpy as jnp
from jax import lax
import numpy as np

D_MODEL = 1024
BATCH = 32
SEQ = 256
DEPTH = 2
DEC_BATCH = 8
DEC_SEQ = 1024
PAST_LEN = 256

GRID_W = 64
Q_BLOCK = 128
ROPE_THETA = 10000.0
EPS = 1e-6
A_HEADS = 8
A_KV_HEADS = 2
A_HEAD_DIM = 64
A_WIDTH = A_HEADS * A_HEAD_DIM
LRU_WIDTH = 512
LRU_BLOCKS = 8
LRU_BLOCK = LRU_WIDTH // LRU_BLOCKS
CONV_W = 4
LRU_C = 8.0
MLA_HEADS = 8
MLA_NOPE = 64
MLA_ROPE = 32
MLA_QK = MLA_NOPE + MLA_ROPE
MLA_V = 64
Q_LORA = 384
KV_LORA = 256
MLA_WIDTH = MLA_HEADS * MLA_V
MIX_WIDTH = A_WIDTH + LRU_WIDTH + MLA_WIDTH
SPLIT_SIZES = (A_WIDTH, A_KV_HEADS * A_HEAD_DIM, A_KV_HEADS * A_HEAD_DIM, LRU_WIDTH, LRU_WIDTH, Q_LORA, KV_LORA, MLA_ROPE)
SPLIT_IDX = (768 - 256 - 0 + 0 - 0, 640, 768, 1280, 1792, 2176, 2432)
D_IN = 2464
N_EXPERTS = 32
TOP_K = 4
D_FF = 1024
SWIGLU_LIMIT = 7.0
SWIGLU_ALPHA = 1.702

kernel_name = 'hybrid_dit_gqa_rglru_mla_moe_step'


def rms_norm(x, g):
    xf = x.astype(jnp.float32)
    y = xf * lax.rsqrt(jnp.mean(xf * xf, axis=-1, keepdims=True) + EPS)
    return (y * g.astype(jnp.float32)).astype(x.dtype)


def ada_mods(cvec, w, b):
    m = jax.nn.silu(cvec) @ w + b
    return [t[:, None, :] for t in jnp.split(m, 6, axis=-1)]


def axial_rope(n, rot_dim):
    rows = n // GRID_W
    r = jnp.repeat(jnp.arange(rows, dtype=jnp.float32), GRID_W)
    col = jnp.tile(jnp.arange(GRID_W, dtype=jnp.float32), rows)
    n_freq = rot_dim // 4
    inv = ROPE_THETA ** (-jnp.arange(n_freq, dtype=jnp.float32) / n_freq)
    ang = jnp.concatenate([r[:, None] * inv, col[:, None] * inv], axis=-1)
    return jnp.cos(ang), jnp.sin(ang)


def apply_rope(x, cs):
    cos, sin = cs
    xf = x.astype(jnp.float32).reshape(x.shape[:-1] + (x.shape[-1] // 2, 2))
    x1, x2 = xf[..., 0], xf[..., 1]
    cb, sb = cos[None, :, None, :], sin[None, :, None, :]
    out = jnp.stack([x1 * cb - x2 * sb, x1 * sb + x2 * cb], axis=-1).reshape(x.shape)
    return out.astype(x.dtype)


def rope_tail(x, cs, n_keep):
    return jnp.concatenate([x[..., :n_keep], apply_rope(x[..., n_keep:], cs)], axis=-1)


def block_attention(q, k, v):
    b, sq, h, dk = q.shape
    hkv, dv = k.shape[2], v.shape[-1]
    g = h // hkv
    nb = sq // Q_BLOCK
    qb = q.reshape(b, nb, Q_BLOCK, hkv, g, dk).transpose(1, 0, 2, 3, 4, 5)
    scale = dk ** -0.5

    def one_block(qblk):
        s = jnp.einsum('bqhgd,bkhd->bhgqk', qblk, k, preferred_element_type=jnp.float32) * scale
        p = jax.nn.softmax(s, axis=-1).astype(v.dtype)
        return jnp.einsum('bhgqk,bkhd->bqhgd', p, v)

    o = lax.map(one_block, qb)
    return o.transpose(1, 0, 2, 3, 4, 5).reshape(b, sq, h * dv)


def conv_centred(x, w, b):
    n = x.shape[1]
    left = CONV_W // 2
    right = CONV_W - 1 - left
    xp = jnp.pad(x, ((0, 0), (left, right), (0, 0)))
    y = b
    for j in range(CONV_W):
        y = y + xp[:, j:j + n, :] * w[j]
    return y


def block_diag(x, w, b):
    xb = x.reshape(x.shape[:-1] + (LRU_BLOCKS, LRU_BLOCK))
    return jnp.einsum('bnhi,hij->bnhj', xb, w).reshape(x.shape) + b


def rglru_coeffs(x, wa, ba, wx, bx, lam):
    r = jax.nn.sigmoid(block_diag(x, wa, ba).astype(jnp.float32))
    i = jax.nn.sigmoid(block_diag(x, wx, bx).astype(jnp.float32))
    log_a = -LRU_C * r * jax.nn.softplus(-lam.astype(jnp.float32))
    a = jnp.exp(log_a)
    bt = jnp.sqrt(-jnp.expm1(2.0 * log_a)) * i * x.astype(jnp.float32)
    return a, bt


def linear_scan(a, bt, h0, reverse):
    if reverse:
        a = jnp.flip(a, axis=1)
        bt = jnp.flip(bt, axis=1)
    if h0 is not None:
        bt = bt.at[:, 0].add(a[:, 0] * h0.astype(jnp.float32))

    def combine(e1, e2):
        a1, b1 = e1
        a2, b2 = e2
        return a1 * a2, a2 * b1 + b2

    _, h = lax.associative_scan(combine, (a, bt), axis=1)
    if reverse:
        h = jnp.flip(h, axis=1)
    return h


def mla_expand_kv(ckv_n, krope, w_ukv, g_k):
    b, n, _ = ckv_n.shape
    kv = (ckv_n @ w_ukv).reshape(b, n, MLA_HEADS, MLA_NOPE + MLA_V)
    k_nope, v = kv[..., :MLA_NOPE], kv[..., MLA_NOPE:]
    k = jnp.concatenate([k_nope, jnp.broadcast_to(krope[:, :, None, :], (b, n, MLA_HEADS, MLA_ROPE))], axis=-1)
    return rms_norm(k, g_k), v


def mixers(h, p, ctx, rope_a, rope_c):
    b, n, _ = h.shape
    z = h @ p['w_in']
    qa, ka, va, xb, yb, cq, ckv, kr = jnp.split(z, SPLIT_IDX, axis=-1)
    qa = rms_norm(qa.reshape(b, n, A_HEADS, A_HEAD_DIM), p['a_q_g'])
    ka = rms_norm(ka.reshape(b, n, A_KV_HEADS, A_HEAD_DIM), p['a_k_g'])
    va = va.reshape(b, n, A_KV_HEADS, A_HEAD_DIM)
    xc = conv_centred(xb, p['conv_w'], p['conv_b'])
    a_f, b_f = rglru_coeffs(xc, p['lru_wa'][0], p['lru_ba'][0], p['lru_wx'][0], p['lru_bx'][0], p['lru_lam'][0])
    a_b, b_b = rglru_coeffs(xc, p['lru_wa'][1], p['lru_ba'][1], p['lru_wx'][1], p['lru_bx'][1], p['lru_lam'][1])
    q_c = (rms_norm(cq, p['mla_cq_g']) @ p['mla_w_uq']).reshape(b, n, MLA_HEADS, MLA_QK)
    q_c = rms_norm(q_c, p['mla_q_g'])
    ckv_n = rms_norm(ckv, p['mla_ckv_g'])
    k_c, v_c = mla_expand_kv(ckv_n, kr, p['mla_w_ukv'], p['mla_k_g'])
    if ctx is None:
        h_f = linear_scan(a_f, b_f, None, False)
        h_b = linear_scan(a_b, b_b, None, True)
        o_a = block_attention(qa, ka, va)
        o_c = block_attention(q_c, k_c, v_c)
        new_ctx = (ka, va, jnp.stack([h_f[:, -1], h_b[:, 0]], axis=1).astype(h.dtype), ckv_n, kr)
    else:
        c_k, c_v, c_h, c_ckv, c_kr = ctx
        h_f = linear_scan(a_f, b_f, c_h[:, 0], False)
        h_b = linear_scan(a_b, b_b, c_h[:, 1], True)
        qa = apply_rope(qa, rope_a)
        ka = apply_rope(ka, rope_a)
        o_a = block_attention(qa, jnp.concatenate([ka, c_k], axis=1), jnp.concatenate([va, c_v], axis=1))
        q_c = rope_tail(q_c, rope_c, MLA_NOPE)
        k_c = rope_tail(k_c, rope_c, MLA_NOPE)
        k_x, v_x = mla_expand_kv(c_ckv, c_kr, p['mla_w_ukv'], p['mla_k_g'])
        o_c = block_attention(q_c, jnp.concatenate([k_c, k_x], axis=1), jnp.concatenate([v_c, v_x], axis=1))
        new_ctx = None
    o_b = (h_f + h_b).astype(h.dtype) * jax.nn.gelu(yb)
    o = jnp.concatenate([o_a, o_b, o_c], axis=-1) @ p['w_out']
    return o, new_ctx


def moe(h, p):
    b, n, d = h.shape
    t = h.reshape(b * n, d)
    logits = (t @ p['router_w'] + p['router_b']).astype(jnp.float32)
    top_v, top_i = lax.top_k(logits, TOP_K)
    wts = jax.nn.softmax(top_v, axis=-1)
    gates = jnp.sum(jax.nn.one_hot(top_i, N_EXPERTS, dtype=jnp.float32) * wts[..., None], axis=1)

    def expert(acc, xs):
        w1, b1, w2, b2, ge = xs
        gu = t @ w1 + b1
        gt, up = gu[:, :D_FF], gu[:, D_FF:]
        gt = jnp.minimum(gt, SWIGLU_LIMIT)
        up = jnp.clip(up, -SWIGLU_LIMIT, SWIGLU_LIMIT)
        act = gt * jax.nn.sigmoid(SWIGLU_ALPHA * gt) * (up + 1.0)
        y = act @ w2 + b2
        return acc + ge[:, None].astype(y.dtype) * y, None

    acc, _ = lax.scan(expert, jnp.zeros_like(t), (p['moe_w1'], p['moe_b1'], p['moe_w2'], p['moe_b2'], gates.T))
    return acc.reshape(b, n, d)


def trunk_layer(x, mods, p, ctx, rope_a, rope_c):
    sh1, sc1, g1, sh2, sc2, g2 = mods
    h = rms_norm(x, p['norm1_g']) * (1.0 + sc1) + sh1
    o, new_ctx = mixers(h, p, ctx, rope_a, rope_c)
    x = x + g1 * o
    h = rms_norm(x, p['norm2_g']) * (1.0 + sc2) + sh2
    x = x + g2 * moe(h, p)
    return x, new_ctx


def setup_inputs(seed: int = 0) -> dict:
    key = jax.random.key(seed)
    ks = jax.random.split(key, 40)
    f32 = jnp.float32

    def nrm(k, shape, scale):
        return jax.random.normal(k, shape, f32) * scale

    def gain(k, shape):
        return 1.0 + 0.02 * jax.random.normal(k, shape, f32)

    u = jax.random.uniform(ks[39], (DEPTH, 2, LRU_WIDTH), f32, minval=0.9, maxval=0.999)
    a0 = u ** (1.0 / LRU_C)
    lam = jnp.log(a0) - jnp.log1p(-a0)
    return {
        'x_prompt': nrm(ks[0], (BATCH, SEQ, D_MODEL), 1.0),
        'x_sample': nrm(ks[1], (DEC_BATCH, DEC_SEQ, D_MODEL), 1.0),
        'cache_attn_k': nrm(ks[2], (DEC_BATCH, DEPTH, PAST_LEN, A_KV_HEADS, A_HEAD_DIM), 1.0),
        'cache_attn_v': nrm(ks[3], (DEC_BATCH, DEPTH, PAST_LEN, A_KV_HEADS, A_HEAD_DIM), 1.0),
        'state_lru': nrm(ks[4], (DEC_BATCH, DEPTH, 2, LRU_WIDTH), 0.5),
        'cache_mla_ckv': nrm(ks[5], (DEC_BATCH, DEPTH, PAST_LEN, KV_LORA), 1.0),
        'cache_mla_krope': nrm(ks[6], (DEC_BATCH, DEPTH, PAST_LEN, MLA_ROPE), 1.0),
        'c': nrm(ks[7], (DEC_BATCH, D_MODEL), 1.0),
        'c_ctx': nrm(ks[8], (D_MODEL,), 1.0),
        'ada_w': nrm(ks[9], (DEPTH, D_MODEL, 6 * D_MODEL), 0.5 * D_MODEL ** -0.5),
        'ada_b': nrm(ks[10], (DEPTH, 6 * D_MODEL), 0.01),
        'norm1_g': gain(ks[11], (DEPTH, D_MODEL)),
        'norm2_g': gain(ks[12], (DEPTH, D_MODEL)),
        'w_in': nrm(ks[13], (DEPTH, D_MODEL, D_IN), D_MODEL ** -0.5),
        'a_q_g': gain(ks[14], (DEPTH, A_HEAD_DIM)),
        'a_k_g': gain(ks[15], (DEPTH, A_HEAD_DIM)),
        'conv_w': nrm(ks[16], (DEPTH, CONV_W, LRU_WIDTH), CONV_W ** -0.5),
        'conv_b': nrm(ks[17], (DEPTH, LRU_WIDTH), 0.01),
        'lru_wa': nrm(ks[18], (DEPTH, 2, LRU_BLOCKS, LRU_BLOCK, LRU_BLOCK), LRU_BLOCK ** -0.5),
        'lru_ba': nrm(ks[19], (DEPTH, 2, LRU_WIDTH), 0.01),
        'lru_wx': nrm(ks[20], (DEPTH, 2, LRU_BLOCKS, LRU_BLOCK, LRU_BLOCK), LRU_BLOCK ** -0.5),
        'lru_bx': nrm(ks[21], (DEPTH, 2, LRU_WIDTH), 0.01),
        'lru_lam': lam,
        'mla_cq_g': gain(ks[22], (DEPTH, Q_LORA)),
        'mla_w_uq': nrm(ks[23], (DEPTH, Q_LORA, MLA_HEADS * MLA_QK), Q_LORA ** -0.5),
        'mla_q_g': gain(ks[24], (DEPTH, MLA_QK)),
        'mla_ckv_g': gain(ks[25], (DEPTH, KV_LORA)),
        'mla_w_ukv': nrm(ks[26], (DEPTH, KV_LORA, MLA_HEADS * (MLA_NOPE + MLA_V)), KV_LORA ** -0.5),
        'mla_k_g': gain(ks[27], (DEPTH, MLA_QK)),
        'w_out': nrm(ks[28], (DEPTH, MIX_WIDTH, D_MODEL), MIX_WIDTH ** -0.5),
        'router_w': nrm(ks[29], (DEPTH, D_MODEL, N_EXPERTS), D_MODEL ** -0.5),
        'router_b': nrm(ks[30], (DEPTH, N_EXPERTS), 0.01),
        'moe_w1': nrm(ks[31], (DEPTH, N_EXPERTS, D_MODEL, 2 * D_FF), D_MODEL ** -0.5),
        'moe_b1': nrm(ks[32], (DEPTH, N_EXPERTS, 2 * D_FF), 0.01),
        'moe_w2': nrm(ks[33], (DEPTH, N_EXPERTS, D_FF, D_MODEL), D_FF ** -0.5),
        'moe_b2': nrm(ks[34], (DEPTH, N_EXPERTS, D_MODEL), 0.01),
    }


def reference(x_prompt, x_sample, cache_attn_k, cache_attn_v, state_lru, cache_mla_ckv, cache_mla_krope, c, c_ctx,
              ada_w, ada_b, norm1_g, norm2_g, w_in, a_q_g, a_k_g, conv_w, conv_b, lru_wa, lru_ba, lru_wx, lru_bx,
              lru_lam, mla_cq_g, mla_w_uq, mla_q_g, mla_ckv_g, mla_w_ukv, mla_k_g, w_out, router_w, router_b,
              moe_w1, moe_b1, moe_w2, moe_b2):
    n_lat = x_sample.shape[1]
    rope_a = axial_rope(n_lat, A_HEAD_DIM)
    rope_c = axial_rope(n_lat, MLA_ROPE)
    yp = x_prompt
    ys = x_sample
    ks_l, vs_l, hs_l, ckv_l, kr_l = [], [], [], [], []
    for l in range(DEPTH):
        p = {
            'w_in': w_in[l], 'a_q_g': a_q_g[l], 'a_k_g': a_k_g[l], 'conv_w': conv_w[l], 'conv_b': conv_b[l],
            'lru_wa': lru_wa[l], 'lru_ba': lru_ba[l], 'lru_wx': lru_wx[l], 'lru_bx': lru_bx[l], 'lru_lam': lru_lam[l],
            'mla_cq_g': mla_cq_g[l], 'mla_w_uq': mla_w_uq[l], 'mla_q_g': mla_q_g[l], 'mla_ckv_g': mla_ckv_g[l],
            'mla_w_ukv': mla_w_ukv[l], 'mla_k_g': mla_k_g[l], 'w_out': w_out[l], 'norm1_g': norm1_g[l],
            'norm2_g': norm2_g[l], 'router_w': router_w[l], 'router_b': router_b[l], 'moe_w1': moe_w1[l],
            'moe_b1': moe_b1[l], 'moe_w2': moe_w2[l], 'moe_b2': moe_b2[l],
        }
        mods_ctx = ada_mods(c_ctx[None, :], ada_w[l], ada_b[l])
        yp, new_ctx = trunk_layer(yp, mods_ctx, p, None, rope_a, rope_c)
        ks_l.append(new_ctx[0])
        vs_l.append(new_ctx[1])
        hs_l.append(new_ctx[2])
        ckv_l.append(new_ctx[3])
        kr_l.append(new_ctx[4])
        mods_lat = ada_mods(c, ada_w[l], ada_b[l])
        ctx = (cache_attn_k[:, l], cache_attn_v[:, l], state_lru[:, l], cache_mla_ckv[:, l], cache_mla_krope[:, l])
        ys, _ = trunk_layer(ys, mods_lat, p, ctx, rope_a, rope_c)
    new_attn_k = jnp.stack(ks_l, axis=1)
    new_attn_v = jnp.stack(vs_l, axis=1)
    new_lru_state = jnp.stack(hs_l, axis=1)
    new_mla_ckv = jnp.stack(ckv_l, axis=1)
    new_mla_krope = jnp.stack(kr_l, axis=1)
    return (yp, ys, new_attn_k, new_attn_v, new_lru_state, new_mla_ckv, new_mla_krope)
```

```python
import functools

import jax
import jax.numpy as jnp
from jax import lax
from jax.experimental import pallas as pl
from jax.experimental.pallas import tpu as pltpu

D_MODEL = 1024
DEPTH = 2
GRID_W = 64
ROPE_THETA = 10000.0
EPS = 1e-6
A_HEADS = 8
A_KV_HEADS = 2
A_HEAD_DIM = 64
LRU_WIDTH = 512
LRU_BLOCKS = 8
CONV_W = 4
LRU_C = 8.0
MLA_HEADS = 8
MLA_NOPE = 64
MLA_ROPE = 32
MLA_QK = MLA_NOPE + MLA_ROPE
MLA_V = 64
Q_LORA = 384
KV_LORA = 256
N_EXPERTS = 32
TOP_K = 4
D_FF = 1024
SWIGLU_LIMIT = 7.0
SWIGLU_ALPHA = 1.702

LANES = 128
N_MODS = 16
VMEM_LIMIT = 56 * 1024 * 1024

_QA0 = 0
_KA0 = _QA0 + A_HEADS * LANES
_VA0 = _KA0 + A_KV_HEADS * LANES
_XB0 = _VA0 + A_KV_HEADS * A_HEAD_DIM
_YB0 = _XB0 + LRU_WIDTH
_CQ0 = _YB0 + LRU_WIDTH
_CKV0 = _CQ0 + Q_LORA
_KR0 = _CKV0 + KV_LORA
_NP = _KR0 + LANES

_BF = jnp.bfloat16
_F32 = jnp.float32


def _cparams(*sem):
    return pltpu.CompilerParams(dimension_semantics=sem, vmem_limit_bytes=VMEM_LIMIT)


def _split_dot(a, b):
    a_hi = a.astype(_BF)
    a_lo = (a - a_hi.astype(_F32)).astype(_BF)
    b_hi = b.astype(_BF)
    b_lo = (b - b_hi.astype(_F32)).astype(_BF)
    d = functools.partial(jnp.dot, preferred_element_type=_F32)
    return d(a_hi, b_hi) + (d(a_hi, b_lo) + d(a_lo, b_hi))


def _rope_rotate(x, cos, sin_signed):
    lane = lax.broadcasted_iota(jnp.int32, x.shape, 1)
    swapped = jnp.where((lane & 1) == 0, pltpu.roll(x, LANES - 1, 1), pltpu.roll(x, 1, 1))
    return x * cos + swapped * sin_signed


def _head_rms(xh, denom):
    return xh * lax.rsqrt(jnp.sum(xh * xh, axis=-1, keepdims=True) * (1.0 / denom) + EPS)


def _mods_kernel(c_ref, w_ref, b_ref, o_ref):
    c = c_ref[...]
    s = c * jax.nn.sigmoid(c)
    o_ref[...] = _split_dot(s, w_ref[...]) + b_ref[...]


def _ada_mods(cvec, ada_w, ada_b):
    tn = 1536
    n_out = ada_w.shape[-1]
    out = pl.pallas_call(
        _mods_kernel,
        grid=(DEPTH, n_out // tn),
        in_specs=[
            pl.BlockSpec((N_MODS, D_MODEL), lambda l, j: (0, 0)),
            pl.BlockSpec((None, D_MODEL, tn), lambda l, j: (l, 0, j)),
            pl.BlockSpec((None, 1, tn), lambda l, j: (l, 0, j)),
        ],
        out_specs=pl.BlockSpec((None, N_MODS, tn), lambda l, j: (l, 0, j)),
        out_shape=jax.ShapeDtypeStruct((DEPTH, N_MODS, n_out), _F32),
        compiler_params=_cparams("arbitrary", "arbitrary"),
        name="ada_mods",
    )(cvec, ada_w, ada_b.reshape(DEPTH, 1, n_out))
    return out.reshape(DEPTH, N_MODS, 6, D_MODEL)


def _proj_kernel(*refs, rope):
    if rope:
        (x_ref, mod_ref, n1g_ref, win_ref, qg_ref, kg_ref, cqg_ref, wuq_ref, mqg_ref, ckvg_ref,
         cosa_ref, sina_ref, cosc_ref, sinc_ref,
         qa_ref, kab_ref, kaf_ref, vab_ref, vaf_ref, xb_ref, yb_ref, qc_ref, ckvn_ref, kr128_ref, kr_ref) = refs
    else:
        (x_ref, mod_ref, n1g_ref, win_ref, qg_ref, kg_ref, cqg_ref, wuq_ref, mqg_ref, ckvg_ref,
         qa_ref, kab_ref, kaf_ref, vab_ref, vaf_ref, xb_ref, yb_ref, qc_ref, ckvn_ref, kr128_ref, kr_ref) = refs
    x = x_ref[...]
    xn = x * lax.rsqrt(jnp.mean(x * x, axis=-1, keepdims=True) + EPS) * n1g_ref[...]
    h = xn * (1.0 + mod_ref[1:2, :]) + mod_ref[0:1, :]
    z = jnp.dot(h.astype(_BF), win_ref[...], preferred_element_type=_F32)

    for hd in range(A_HEADS):
        q = _head_rms(z[:, _QA0 + hd * LANES:_QA0 + (hd + 1) * LANES], A_HEAD_DIM) * qg_ref[...]
        if rope:
            q = _rope_rotate(q, cosa_ref[...], sina_ref[...])
        qa_ref[:, hd * LANES:(hd + 1) * LANES] = (q * (A_HEAD_DIM ** -0.5)).astype(_BF)
    k_heads = []
    for hd in range(A_KV_HEADS):
        k = _head_rms(z[:, _KA0 + hd * LANES:_KA0 + (hd + 1) * LANES], A_HEAD_DIM) * kg_ref[...]
        k_heads.append(k[:, :A_HEAD_DIM])
        if rope:
            k = _rope_rotate(k, cosa_ref[...], sina_ref[...])
        kab_ref[:, hd * LANES:(hd + 1) * LANES] = k.astype(_BF)
    kaf_ref[...] = jnp.concatenate(k_heads, axis=-1)
    va = z[:, _VA0:_VA0 + A_KV_HEADS * A_HEAD_DIM]
    vaf_ref[...] = va
    vab_ref[...] = va.astype(_BF)

    xb_ref[...] = z[:, _XB0:_XB0 + LRU_WIDTH]
    yb_ref[...] = z[:, _YB0:_YB0 + LRU_WIDTH]

    cq = z[:, _CQ0:_CQ0 + Q_LORA]
    cqn = cq * lax.rsqrt(jnp.mean(cq * cq, axis=-1, keepdims=True) + EPS) * cqg_ref[...]
    zq = jnp.dot(cqn.astype(_BF), wuq_ref[...], preferred_element_type=_F32)
    for hd in range(MLA_HEADS):
        q = _head_rms(zq[:, hd * LANES:(hd + 1) * LANES], MLA_QK) * mqg_ref[...]
        if rope:
            q = _rope_rotate(q, cosc_ref[...], sinc_ref[...])
        qc_ref[:, hd * LANES:(hd + 1) * LANES] = (q * (MLA_QK ** -0.5)).astype(_BF)
    ckv = z[:, _CKV0:_CKV0 + KV_LORA]
    ckvn_ref[...] = ckv * lax.rsqrt(jnp.mean(ckv * ckv, axis=-1, keepdims=True) + EPS) * ckvg_ref[...]
    kr128 = z[:, _KR0:_KR0 + LANES]
    kr128_ref[...] = kr128
    kr_ref[...] = kr128[:, MLA_NOPE:MLA_NOPE + MLA_ROPE]


def _proj(x, mods_l, wts, tabs, *, rope, tm, seq):
    t = x.shape[0]
    nt = t // tm
    per_seq = seq // tm

    def row(i):
        return (i, 0)

    def const2(i):
        return (0, 0)

    if rope:
        def mod_idx(i):
            return (1 + i // per_seq, 0, 0)

        def tab_idx(i):
            return (i % per_seq, 0)
    else:
        def mod_idx(i):
            return (0, 0, 0)

    in_specs = [
        pl.BlockSpec((tm, D_MODEL), row),
        pl.BlockSpec((None, 6, D_MODEL), mod_idx),
        pl.BlockSpec((1, D_MODEL), const2),
        pl.BlockSpec((D_MODEL, _NP), const2),
        pl.BlockSpec((1, LANES), const2),
        pl.BlockSpec((1, LANES), const2),
        pl.BlockSpec((1, Q_LORA), const2),
        pl.BlockSpec((Q_LORA, MLA_HEADS * LANES), const2),
        pl.BlockSpec((1, LANES), const2),
        pl.BlockSpec((1, KV_LORA), const2),
    ]
    args = [x, mods_l, wts["norm1_g"], wts["w_in"], wts["a_q_g"], wts["a_k_g"], wts["mla_cq_g"], wts["w_uq"],
            wts["mla_q_g"], wts["mla_ckv_g"]]
    if rope:
        in_specs += [pl.BlockSpec((tm, LANES), tab_idx)] * 4
        args += list(tabs)
    widths = [(A_HEADS * LANES, _BF), (A_KV_HEADS * LANES, _BF), (A_KV_HEADS * A_HEAD_DIM, _F32),
              (A_KV_HEADS * A_HEAD_DIM, _BF), (A_KV_HEADS * A_HEAD_DIM, _F32), (LRU_WIDTH, _F32), (LRU_WIDTH, _F32),
              (MLA_HEADS * LANES, _BF), (KV_LORA, _F32), (LANES, _F32), (MLA_ROPE, _F32)]
    out_specs = [pl.BlockSpec((tm, w), row) for w, _ in widths]
    out_shape = [jax.ShapeDtypeStruct((t, w), dt) for w, dt in widths]
    return pl.pallas_call(
        functools.partial(_proj_kernel, rope=rope),
        grid=(nt,),
        in_specs=in_specs,
        out_specs=out_specs,
        out_shape=out_shape,
        compiler_params=_cparams("arbitrary"),
        name="proj_rope" if rope else "proj",
    )(*args)


def _mlakv_kernel(*refs, rope):
    if rope:
        ckvn_ref, kr128_ref, wukv_ref, kg_ref, cos_ref, sin_ref, k_ref, v_ref = refs
    else:
        ckvn_ref, kr128_ref, wukv_ref, kg_ref, k_ref, v_ref = refs
    kv = jnp.dot(ckvn_ref[...].astype(_BF), wukv_ref[...], preferred_element_type=_F32)
    kr128 = kr128_ref[...]
    for hd in range(MLA_HEADS):
        k = _head_rms(kv[:, hd * LANES:(hd + 1) * LANES] + kr128, MLA_QK) * kg_ref[...]
        if rope:
            k = _rope_rotate(k, cos_ref[...], sin_ref[...])
        k_ref[:, hd * LANES:(hd + 1) * LANES] = k.astype(_BF)
    v_ref[...] = kv[:, MLA_HEADS * LANES:].astype(_BF)


def _mlakv(ckvn, kr128, wts, tabs, *, rope, tm, seq):
    t = ckvn.shape[0]
    per_seq = seq // tm
    in_specs = [
        pl.BlockSpec((tm, KV_LORA), lambda i: (i, 0)),
        pl.BlockSpec((tm, LANES), lambda i: (i, 0)),
        pl.BlockSpec((KV_LORA, MLA_HEADS * (LANES + MLA_V)), lambda i: (0, 0)),
        pl.BlockSpec((1, LANES), lambda i: (0, 0)),
    ]
    args = [ckvn, kr128, wts["w_ukv"], wts["mla_k_g"]]
    if rope:
        in_specs += [pl.BlockSpec((tm, LANES), lambda i: (i % per_seq, 0))] * 2
        args += [tabs[2], tabs[3]]
    return pl.pallas_call(
        functools.partial(_mlakv_kernel, rope=rope),
        grid=(t // tm,),
        in_specs=in_specs,
        out_specs=[pl.BlockSpec((tm, MLA_HEADS * LANES), lambda i: (i, 0)),
                   pl.BlockSpec((tm, MLA_HEADS * MLA_V), lambda i: (i, 0))],
        out_shape=[jax.ShapeDtypeStruct((t, MLA_HEADS * LANES), _BF),
                   jax.ShapeDtypeStruct((t, MLA_HEADS * MLA_V), _BF)],
        compiler_params=_cparams("arbitrary"),
        name="mlakv_rope" if rope else "mlakv",
    )(*args)


_CHUNK = 8


def _lru_kernel(xb_ref, yb_ref, h0_ref, cw_ref, cb_ref, wg_ref, bg_ref, lam_ref,
                ob_ref, st_ref, af_ref, bf_ref, ab_ref, bb_ref, h_ref, *, n):
    x = xb_ref[...]
    rows = lax.broadcasted_iota(jnp.int32, x.shape, 0)
    xc = cb_ref[...] + x * cw_ref[2:3, :]
    xc = xc + jnp.where(rows >= 2, pltpu.roll(x, 2, 0), 0.0) * cw_ref[0:1, :]
    xc = xc + jnp.where(rows >= 1, pltpu.roll(x, 1, 0), 0.0) * cw_ref[1:2, :]
    xc = xc + jnp.where(rows < n - 1, pltpu.roll(x, n - 1, 0), 0.0) * cw_ref[3:4, :]

    xcb = xc.astype(_BF)
    half = LRU_WIDTH // 2
    pre = [jnp.dot(xcb[:, j * half:(j + 1) * half], wg_ref[j], preferred_element_type=_F32) for j in range(2)]
    for d, (a_ref, b_ref) in enumerate(((af_ref, bf_ref), (ab_ref, bb_ref))):
        pa = jnp.concatenate([pre[0][:, (2 * d) * half:(2 * d + 1) * half],
                              pre[1][:, (2 * d) * half:(2 * d + 1) * half]], axis=-1)
        px = jnp.concatenate([pre[0][:, (2 * d + 1) * half:(2 * d + 2) * half],
                              pre[1][:, (2 * d + 1) * half:(2 * d + 2) * half]], axis=-1)
        r = jax.nn.sigmoid(pa + bg_ref[2 * d:2 * d + 1, :])
        i = jax.nn.sigmoid(px + bg_ref[2 * d + 1:2 * d + 2, :])
        nlam = -lam_ref[d:d + 1, :]
        softplus = jnp.maximum(nlam, 0.0) + jnp.log1p(jnp.exp(-jnp.abs(nlam)))
        log_a = (-LRU_C) * r * softplus
        a = jnp.exp(log_a)
        a_ref[...] = a
        b_ref[...] = jnp.sqrt(-jnp.tanh(log_a) * (a * a + 1.0)) * i * xc

    nchunks = n // _CHUNK
    crow = lax.broadcasted_iota(jnp.int32, (_CHUNK, LRU_WIDTH), 0)

    def chunk_scan(a, b, carry, reverse):
        for d in (1, 2, 4):
            if reverse:
                keep = crow < _CHUNK - d
                a_s = pltpu.roll(a, _CHUNK - d, 0)
                b_s = pltpu.roll(b, _CHUNK - d, 0)
            else:
                keep = crow >= d
                a_s = pltpu.roll(a, d, 0)
                b_s = pltpu.roll(b, d, 0)
            b = jnp.where(keep, a * b_s + b, b)
            a = jnp.where(keep, a * a_s, a)
        return a * carry + b

    def body(ci, carry):
        cf, cb = carry
        off_f = pl.multiple_of(ci * _CHUNK, _CHUNK)
        off_b = pl.multiple_of((nchunks - 1 - ci) * _CHUNK, _CHUNK)
        hf = chunk_scan(af_ref[pl.ds(off_f, _CHUNK), :], bf_ref[pl.ds(off_f, _CHUNK), :], cf, False)
        hb = chunk_scan(ab_ref[pl.ds(off_b, _CHUNK), :], bb_ref[pl.ds(off_b, _CHUNK), :], cb, True)
        bf_ref[pl.ds(off_f, _CHUNK), :] = hf
        bb_ref[pl.ds(off_b, _CHUNK), :] = hb
        return hf[_CHUNK - 1:_CHUNK, :], hb[0:1, :]

    cf, cb = lax.fori_loop(0, nchunks, body, (h0_ref[0:1, :], h0_ref[1:2, :]))
    st_ref[0:1, :] = cf
    st_ref[1:2, :] = cb
    ob_ref[...] = ((bf_ref[...] + bb_ref[...]) * jax.nn.gelu(yb_ref[...])).astype(_BF)


def _lru(xb, yb, h0, wts, *, n):
    nb = xb.shape[0] // n
    seq = lambda b: (b, 0)
    const2 = lambda b: (0, 0)
    return pl.pallas_call(
        functools.partial(_lru_kernel, n=n),
        grid=(nb,),
        in_specs=[
            pl.BlockSpec((n, LRU_WIDTH), seq),
            pl.BlockSpec((n, LRU_WIDTH), seq),
            pl.BlockSpec((None, 2, LRU_WIDTH), lambda b: (b, 0, 0)),
            pl.BlockSpec((CONV_W, LRU_WIDTH), const2),
            pl.BlockSpec((1, LRU_WIDTH), const2),
            pl.BlockSpec((2, LRU_WIDTH // 2, 2 * LRU_WIDTH), lambda b: (0, 0, 0)),
            pl.BlockSpec((4, LRU_WIDTH), const2),
            pl.BlockSpec((2, LRU_WIDTH), const2),
        ],
        out_specs=[pl.BlockSpec((n, LRU_WIDTH), seq),
                   pl.BlockSpec((None, 2, LRU_WIDTH), lambda b: (b, 0, 0))],
        out_shape=[jax.ShapeDtypeStruct((nb * n, LRU_WIDTH), _BF),
                   jax.ShapeDtypeStruct((nb, 2, LRU_WIDTH), _F32)],
        scratch_shapes=[pltpu.VMEM((n, LRU_WIDTH), _F32) for _ in range(5)],
        compiler_params=_cparams("arbitrary"),
        name=f"lru_{n}",
    )(xb, yb, h0, wts["conv_w"], wts["conv_b"], wts["lru_wg"], wts["lru_bg"], wts["lru_lam"])


def _attn_kernel(*refs, heads, kv_heads, dv, cached):
    if cached:
        q_ref, k_ref, v_ref, kc_ref, vc_ref, o_ref = refs
    else:
        q_ref, k_ref, v_ref, o_ref = refs
    group = heads // kv_heads
    contract_last = (((1,), (1,)), ((), ()))
    outs = []
    for hd in range(heads):
        g = hd // group
        q = q_ref[:, hd * LANES:(hd + 1) * LANES]
        s = lax.dot_general(q, k_ref[:, g * LANES:(g + 1) * LANES], contract_last, preferred_element_type=_F32)
        m = jnp.max(s, axis=-1, keepdims=True)
        if cached:
            sc = lax.dot_general(q, kc_ref[:, g * LANES:(g + 1) * LANES], contract_last,
                                 preferred_element_type=_F32)
            m = jnp.maximum(m, jnp.max(sc, axis=-1, keepdims=True))
        p = jnp.exp(s - m)
        den = jnp.sum(p, axis=-1, keepdims=True)
        o = jnp.dot(p.astype(_BF), v_ref[:, g * dv:(g + 1) * dv], preferred_element_type=_F32)
        if cached:
            pc = jnp.exp(sc - m)
            den = den + jnp.sum(pc, axis=-1, keepdims=True)
            o = o + jnp.dot(pc.astype(_BF), vc_ref[:, g * dv:(g + 1) * dv], preferred_element_type=_F32)
        outs.append(o / den)
    o_ref[...] = jnp.concatenate(outs, axis=-1).astype(_BF)


def _attn(q, k, v, kc, vc, *, heads, kv_heads, dv, n, tq, name):
    t = q.shape[0]
    nb = t // n
    nq = n // tq
    cached = kc is not None
    in_specs = [
        pl.BlockSpec((tq, heads * LANES), lambda b, i: (b * nq + i, 0)),
        pl.BlockSpec((n, kv_heads * LANES), lambda b, i: (b, 0)),
        pl.BlockSpec((n, kv_heads * dv), lambda b, i: (b, 0)),
    ]
    args = [q, k, v]
    if cached:
        nc = kc.shape[0] // nb
        in_specs += [pl.BlockSpec((nc, kv_heads * LANES), lambda b, i: (b, 0)),
                     pl.BlockSpec((nc, kv_heads * dv), lambda b, i: (b, 0))]
        args += [kc, vc]
    return pl.pallas_call(
        functools.partial(_attn_kernel, heads=heads, kv_heads=kv_heads, dv=dv, cached=cached),
        grid=(nb, nq),
        in_specs=in_specs,
        out_specs=pl.BlockSpec((tq, heads * dv), lambda b, i: (b * nq + i, 0)),
        out_shape=jax.ShapeDtypeStruct((t, heads * dv), _BF),
        compiler_params=_cparams("arbitrary", "arbitrary"),
        name=name,
    )(*args)


def _post_kernel(oa_ref, ob_ref, oc_ref, x_ref, mod_ref, wout_ref, n2g_ref, rw_ref, rb_ref,
                 x1_ref, h2_ref, gates_ref):
    w0 = A_HEADS * A_HEAD_DIM
    w1 = w0 + LRU_WIDTH
    o = jnp.dot(oa_ref[...], wout_ref[0:w0, :], preferred_element_type=_F32)
    o = o + jnp.dot(ob_ref[...], wout_ref[w0:w1, :], preferred_element_type=_F32)
    o = o + jnp.dot(oc_ref[...], wout_ref[w1:, :], preferred_element_type=_F32)
    x1 = x_ref[...] + mod_ref[2:3, :] * o
    x1_ref[...] = x1
    xn = x1 * lax.rsqrt(jnp.mean(x1 * x1, axis=-1, keepdims=True) + EPS) * n2g_ref[...]
    h2 = xn * (1.0 + mod_ref[4:5, :]) + mod_ref[3:4, :]
    h2_ref[...] = h2.astype(_BF)

    logits = _split_dot(h2, rw_ref[...]) + rb_ref[...]
    lane = lax.broadcasted_iota(jnp.int32, logits.shape, 1).astype(_F32)
    neg = jnp.float32(-jnp.inf)
    work = logits
    top = None
    for _ in range(TOP_K):
        m = jnp.max(work, axis=-1, keepdims=True)
        if top is None:
            top = m
        first = jnp.min(jnp.where(work == m, lane, float(LANES)), axis=-1, keepdims=True)
        work = jnp.where(lane == first, neg, work)
    e = jnp.where(work == neg, jnp.exp(logits - top), 0.0)
    gates_ref[...] = e / jnp.sum(e, axis=-1, keepdims=True)


def _post(oa, ob, oc, x, mods_l, wts, *, rope, tm, seq):
    t = x.shape[0]
    per_seq = seq // tm
    row = lambda i: (i, 0)
    const2 = lambda i: (0, 0)
    mod_idx = (lambda i: (1 + i // per_seq, 0, 0)) if rope else (lambda i: (0, 0, 0))
    mix = A_HEADS * A_HEAD_DIM + LRU_WIDTH + MLA_HEADS * MLA_V
    return pl.pallas_call(
        _post_kernel,
        grid=(t // tm,),
        in_specs=[
            pl.BlockSpec((tm, A_HEADS * A_HEAD_DIM), row),
            pl.BlockSpec((tm, LRU_WIDTH), row),
            pl.BlockSpec((tm, MLA_HEADS * MLA_V), row),
            pl.BlockSpec((tm, D_MODEL), row),
            pl.BlockSpec((None, 6, D_MODEL), mod_idx),
            pl.BlockSpec((mix, D_MODEL), const2),
            pl.BlockSpec((1, D_MODEL), const2),
            pl.BlockSpec((D_MODEL, LANES), const2),
            pl.BlockSpec((1, LANES), const2),
        ],
        out_specs=[pl.BlockSpec((tm, D_MODEL), row), pl.BlockSpec((tm, D_MODEL), row),
                   pl.BlockSpec((tm, LANES), row)],
        out_shape=[jax.ShapeDtypeStruct((t, D_MODEL), _F32), jax.ShapeDtypeStruct((t, D_MODEL), _BF),
                   jax.ShapeDtypeStruct((t, LANES), _F32)],
        compiler_params=_cparams("arbitrary"),
        name="post",
    )(oa, ob, oc, x, mods_l, wts["w_out"], wts["norm2_g"], wts["router_w"], wts["router_b"])


def _moe_kernel(h2_ref, gates_ref, x1_ref, mod_ref, w1_ref, b1_ref, w2_ref, b2_ref, o_ref, acc_ref):
    e = pl.program_id(1)

    @pl.when(e == 0)
    def _():
        acc_ref[...] = jnp.zeros_like(acc_ref)

    gu = jnp.dot(h2_ref[...], w1_ref[...], preferred_element_type=_F32) + b1_ref[...]
    gt = jnp.minimum(gu[:, :D_FF], SWIGLU_LIMIT)
    up = jnp.clip(gu[:, D_FF:], -SWIGLU_LIMIT, SWIGLU_LIMIT)
    act = gt * jax.nn.sigmoid(SWIGLU_ALPHA * gt) * (up + 1.0)
    y = jnp.dot(act.astype(_BF), w2_ref[...], preferred_element_type=_F32) + b2_ref[...]
    lane = lax.broadcasted_iota(jnp.int32, gates_ref.shape, 1)
    ge = jnp.sum(jnp.where(lane == e, gates_ref[...], 0.0), axis=-1, keepdims=True)
    acc_ref[...] += ge * y

    @pl.when(e == N_EXPERTS - 1)
    def _():
        o_ref[...] = x1_ref[...] + mod_ref[5:6, :] * acc_ref[...]


def _moe(h2, gates, x1, mods_l, wts, *, rope, tm, seq):
    t = h2.shape[0]
    per_seq = seq // tm
    row = lambda i, e: (i, 0)
    mod_idx = (lambda i, e: (1 + i // per_seq, 0, 0)) if rope else (lambda i, e: (0, 0, 0))
    return pl.pallas_call(
        _moe_kernel,
        grid=(t // tm, N_EXPERTS),
        in_specs=[
            pl.BlockSpec((tm, D_MODEL), row),
            pl.BlockSpec((tm, LANES), row),
            pl.BlockSpec((tm, D_MODEL), row),
            pl.BlockSpec((None, 6, D_MODEL), mod_idx),
            pl.BlockSpec((None, D_MODEL, 2 * D_FF), lambda i, e: (e, 0, 0)),
            pl.BlockSpec((None, 1, 2 * D_FF), lambda i, e: (e, 0, 0)),
            pl.BlockSpec((None, D_FF, D_MODEL), lambda i, e: (e, 0, 0)),
            pl.BlockSpec((None, 1, D_MODEL), lambda i, e: (e, 0, 0)),
        ],
        out_specs=pl.BlockSpec((tm, D_MODEL), row),
        out_shape=jax.ShapeDtypeStruct((t, D_MODEL), _F32),
        scratch_shapes=[pltpu.VMEM((tm, D_MODEL), _F32)],
        compiler_params=_cparams("arbitrary", "arbitrary"),
        name="moe",
    )(h2, gates, x1, mods_l, wts["moe_w1"], wts["moe_b1"], wts["moe_w2"], wts["moe_b2"])


def _pad_heads(w, heads, dim):
    lead = w.shape[:-1]
    w = w.reshape(lead + (heads, dim))
    w = jnp.pad(w, [(0, 0)] * len(lead) + [(0, 0), (0, LANES - dim)])
    return w.reshape(lead + (heads * LANES,))


def _pad_lanes(g, left=0):
    return jnp.pad(g, (left, LANES - left - g.shape[0])).reshape(1, LANES)


def _block_diag_halves(w):
    w4 = w.reshape(2, 4, 64, 64)
    eye = jnp.eye(4, dtype=w.dtype)
    return jnp.einsum("jaik,ab->jaibk", w4, eye).reshape(2, 256, 256)


def _layer_weights(l, p):
    w_in = p["w_in"][l]
    qa, ka, va, xb, yb, cq, ckv, kr = jnp.split(w_in, (512, 640, 768, 1280, 1792, 2176, 2432), axis=-1)
    kr128 = jnp.pad(kr, ((0, 0), (MLA_NOPE, LANES - MLA_NOPE - MLA_ROPE)))
    w_in_p = jnp.concatenate([_pad_heads(qa, A_HEADS, A_HEAD_DIM), _pad_heads(ka, A_KV_HEADS, A_HEAD_DIM), va, xb,
                              yb, cq, ckv, kr128], axis=-1).astype(_BF)
    ukv = p["mla_w_ukv"][l].reshape(KV_LORA, MLA_HEADS, MLA_NOPE + MLA_V)
    w_ukv = jnp.concatenate([_pad_heads(ukv[:, :, :MLA_NOPE].reshape(KV_LORA, -1), MLA_HEADS, MLA_NOPE),
                             ukv[:, :, MLA_NOPE:].reshape(KV_LORA, -1)], axis=-1).astype(_BF)
    wg = jnp.concatenate([_block_diag_halves(p["lru_wa"][l, 0]), _block_diag_halves(p["lru_wx"][l, 0]),
                          _block_diag_halves(p["lru_wa"][l, 1]), _block_diag_halves(p["lru_wx"][l, 1])],
                         axis=-1).astype(_BF)
    bg = jnp.stack([p["lru_ba"][l, 0], p["lru_bx"][l, 0], p["lru_ba"][l, 1], p["lru_bx"][l, 1]])
    return {
        "norm1_g": p["norm1_g"][l].reshape(1, -1),
        "norm2_g": p["norm2_g"][l].reshape(1, -1),
        "w_in": w_in_p,
        "a_q_g": _pad_lanes(p["a_q_g"][l]),
        "a_k_g": _pad_lanes(p["a_k_g"][l]),
        "mla_cq_g": p["mla_cq_g"][l].reshape(1, -1),
        "w_uq": _pad_heads(p["mla_w_uq"][l], MLA_HEADS, MLA_QK).astype(_BF),
        "mla_q_g": _pad_lanes(p["mla_q_g"][l]),
        "mla_ckv_g": p["mla_ckv_g"][l].reshape(1, -1),
        "w_ukv": w_ukv,
        "mla_k_g": _pad_lanes(p["mla_k_g"][l]),
        "conv_w": p["conv_w"][l],
        "conv_b": p["conv_b"][l].reshape(1, -1),
        "lru_wg": wg,
        "lru_bg": bg,
        "lru_lam": p["lru_lam"][l],
        "w_out": p["w_out"][l].astype(_BF),
        "router_w": jnp.pad(p["router_w"][l], ((0, 0), (0, LANES - N_EXPERTS))),
        "router_b": jnp.pad(p["router_b"][l], (0, LANES - N_EXPERTS), constant_values=-1e30).reshape(1, LANES),
        "moe_w1": p["moe_w1"][l].astype(_BF),
        "moe_b1": p["moe_b1"][l].reshape(N_EXPERTS, 1, 2 * D_FF),
        "moe_w2": p["moe_w2"][l].astype(_BF),
        "moe_b2": p["moe_b2"][l].reshape(N_EXPERTS, 1, D_MODEL),
    }


def _rope_tables(n, rot_dim, lane0):
    rows = n // GRID_W
    r = jnp.repeat(jnp.arange(rows, dtype=_F32), GRID_W)
    col = jnp.tile(jnp.arange(GRID_W, dtype=_F32), rows)
    n_freq = rot_dim // 4
    inv = ROPE_THETA ** (-jnp.arange(n_freq, dtype=_F32) / n_freq)
    ang = jnp.concatenate([r[:, None] * inv, col[:, None] * inv], axis=-1)
    cos = jnp.repeat(jnp.cos(ang), 2, axis=-1)
    sin = jnp.repeat(jnp.sin(ang), 2, axis=-1) * jnp.tile(jnp.array([-1.0, 1.0], _F32), rot_dim // 2)
    pad = ((0, 0), (lane0, LANES - lane0 - rot_dim))
    return jnp.pad(cos, pad, constant_values=1.0), jnp.pad(sin, pad)


def kernel(x_prompt, x_sample, cache_attn_k, cache_attn_v, state_lru, cache_mla_ckv, cache_mla_krope, c, c_ctx,
           ada_w, ada_b, norm1_g, norm2_g, w_in, a_q_g, a_k_g, conv_w, conv_b, lru_wa, lru_ba, lru_wx, lru_bx,
           lru_lam, mla_cq_g, mla_w_uq, mla_q_g, mla_ckv_g, mla_w_ukv, mla_k_g, w_out, router_w, router_b,
           moe_w1, moe_b1, moe_w2, moe_b2):
    p = dict(norm1_g=norm1_g, norm2_g=norm2_g, w_in=w_in, a_q_g=a_q_g, a_k_g=a_k_g, conv_w=conv_w, conv_b=conv_b,
             lru_wa=lru_wa, lru_ba=lru_ba, lru_wx=lru_wx, lru_bx=lru_bx, lru_lam=lru_lam, mla_cq_g=mla_cq_g,
             mla_w_uq=mla_w_uq, mla_q_g=mla_q_g, mla_ckv_g=mla_ckv_g, mla_w_ukv=mla_w_ukv, mla_k_g=mla_k_g,
             w_out=w_out, router_w=router_w, router_b=router_b, moe_w1=moe_w1, moe_b1=moe_b1, moe_w2=moe_w2,
             moe_b2=moe_b2)
    batch, seq, _ = x_prompt.shape
    dec_batch, dec_seq, _ = x_sample.shape
    past = cache_attn_k.shape[2]

    cvec = jnp.concatenate([c_ctx[None, :], c, jnp.zeros((N_MODS - 1 - dec_batch, D_MODEL), _F32)], axis=0)
    mods = _ada_mods(cvec, ada_w, ada_b)
    cos_a, sin_a = _rope_tables(dec_seq, A_HEAD_DIM, 0)
    cos_c, sin_c = _rope_tables(dec_seq, MLA_ROPE, MLA_NOPE)
    tabs = (cos_a, sin_a, cos_c, sin_c)

    yp = x_prompt.reshape(batch * seq, D_MODEL)
    ys = x_sample.reshape(dec_batch * dec_seq, D_MODEL)
    zero_state = jnp.zeros((batch, 2, LRU_WIDTH), _F32)
    ks_l, vs_l, hs_l, ckv_l, kr_l = [], [], [], [], []
    tm = 256
    for l in range(DEPTH):
        wts = _layer_weights(l, p)
        mods_l = mods[l]

        qa, kab, kaf, vab, vaf, xb, yb, qc, ckvn, kr128, kr = _proj(yp, mods_l, wts, None, rope=False, tm=tm, seq=seq)
        kc, vc = _mlakv(ckvn, kr128, wts, None, rope=False, tm=tm, seq=seq)
        ob, st = _lru(xb, yb, zero_state, wts, n=seq)
        oa = _attn(qa, kab, vab, None, None, heads=A_HEADS, kv_heads=A_KV_HEADS, dv=A_HEAD_DIM, n=seq, tq=seq,
                   name="attn_a_ctx")
        oc = _attn(qc, kc, vc, None, None, heads=MLA_HEADS, kv_heads=MLA_HEADS, dv=MLA_V, n=seq, tq=seq,
                   name="attn_c_ctx")
        x1, h2, gates = _post(oa, ob, oc, yp, mods_l, wts, rope=False, tm=tm, seq=seq)
        yp = _moe(h2, gates, x1, mods_l, wts, rope=False, tm=512, seq=seq)
        ks_l.append(kaf.reshape(batch, seq, A_KV_HEADS, A_HEAD_DIM))
        vs_l.append(vaf.reshape(batch, seq, A_KV_HEADS, A_HEAD_DIM))
        hs_l.append(st)
        ckv_l.append(ckvn.reshape(batch, seq, KV_LORA))
        kr_l.append(kr.reshape(batch, seq, MLA_ROPE))

        qa, kab, _, vab, _, xb, yb, qc, ckvn, kr128, _ = _proj(ys, mods_l, wts, tabs, rope=True, tm=tm, seq=dec_seq)
        kc, vc = _mlakv(ckvn, kr128, wts, tabs, rope=True, tm=tm, seq=dec_seq)
        c_kr128 = jnp.pad(cache_mla_krope[:, l].reshape(dec_batch * past, MLA_ROPE),
                          ((0, 0), (MLA_NOPE, LANES - MLA_NOPE - MLA_ROPE)))
        kx, vx = _mlakv(cache_mla_ckv[:, l].reshape(dec_batch * past, KV_LORA), c_kr128, wts, None, rope=False,
                        tm=tm, seq=past)
        ob, _ = _lru(xb, yb, state_lru[:, l], wts, n=dec_seq)
        c_k = _pad_heads(cache_attn_k[:, l].reshape(dec_batch * past, A_KV_HEADS * A_HEAD_DIM), A_KV_HEADS,
                         A_HEAD_DIM).astype(_BF)
        c_v = cache_attn_v[:, l].reshape(dec_batch * past, A_KV_HEADS * A_HEAD_DIM).astype(_BF)
        oa = _attn(qa, kab, vab, c_k, c_v, heads=A_HEADS, kv_heads=A_KV_HEADS, dv=A_HEAD_DIM, n=dec_seq, tq=256,
                   name="attn_a_lat")
        oc = _attn(qc, kc, vc, kx, vx, heads=MLA_HEADS, kv_heads=MLA_HEADS, dv=MLA_V, n=dec_seq, tq=256,
                   name="attn_c_lat")
        x1, h2, gates = _post(oa, ob, oc, ys, mods_l, wts, rope=True, tm=tm, seq=dec_seq)
        ys = _moe(h2, gates, x1, mods_l, wts, rope=True, tm=512, seq=dec_seq)

    return (yp.reshape(batch, seq, D_MODEL), ys.reshape(dec_batch, dec_seq, D_MODEL),
            jnp.stack(ks_l, axis=1), jnp.stack(vs_l, axis=1), jnp.stack(hs_l, axis=1),
            jnp.stack(ckv_l, axis=1), jnp.stack(kr_l, axis=1))
```

```python
import functools

import jax
import jax.numpy as jnp
from jax import lax
from jax.experimental import pallas as pl
from jax.experimental.pallas import tpu as pltpu

D_MODEL = 1024
DEPTH = 2
GRID_W = 64
ROPE_THETA = 10000.0
EPS = 1e-6
A_HEADS = 8
A_KV_HEADS = 2
A_HEAD_DIM = 64
LRU_WIDTH = 512
LRU_BLOCKS = 8
CONV_W = 4
LRU_C = 8.0
MLA_HEADS = 8
MLA_NOPE = 64
MLA_ROPE = 32
MLA_QK = MLA_NOPE + MLA_ROPE
MLA_V = 64
Q_LORA = 384
KV_LORA = 256
N_EXPERTS = 32
TOP_K = 4
D_FF = 1024
SWIGLU_LIMIT = 7.0
SWIGLU_ALPHA = 1.702

LANES = 128
N_MODS = 16
VMEM_LIMIT = 56 * 1024 * 1024

_QA0 = 0
_KA0 = _QA0 + A_HEADS * LANES
_VA0 = _KA0 + A_KV_HEADS * LANES
_XB0 = _VA0 + A_KV_HEADS * A_HEAD_DIM
_YB0 = _XB0 + LRU_WIDTH
_CQ0 = _YB0 + LRU_WIDTH
_CKV0 = _CQ0 + Q_LORA
_KR0 = _CKV0 + KV_LORA
_NP = _KR0 + LANES

_BF = jnp.bfloat16
_F32 = jnp.float32


def _cparams(*sem):
    return pltpu.CompilerParams(dimension_semantics=sem, vmem_limit_bytes=VMEM_LIMIT)


def _split_dot(a, b):
    a_hi = a.astype(_BF)
    a_lo = (a - a_hi.astype(_F32)).astype(_BF)
    b_hi = b.astype(_BF)
    b_lo = (b - b_hi.astype(_F32)).astype(_BF)
    d = functools.partial(jnp.dot, preferred_element_type=_F32)
    return d(a_hi, b_hi) + (d(a_hi, b_lo) + d(a_lo, b_hi))


def _rope_rotate(x, cos, sin_signed):
    lane = lax.broadcasted_iota(jnp.int32, x.shape, 1)
    swapped = jnp.where((lane & 1) == 0, pltpu.roll(x, LANES - 1, 1), pltpu.roll(x, 1, 1))
    return x * cos + swapped * sin_signed


def _head_rms(xh, denom):
    return xh * lax.rsqrt(jnp.sum(xh * xh, axis=-1, keepdims=True) * (1.0 / denom) + EPS)


def _mods_kernel(c_ref, w_ref, b_ref, o_ref):
    c = c_ref[...]
    s = c * jax.nn.sigmoid(c)
    o_ref[...] = _split_dot(s, w_ref[...]) + b_ref[...]


def _ada_mods(cvec, ada_w, ada_b):
    tn = 1536
    n_out = ada_w.shape[-1]
    out = pl.pallas_call(
        _mods_kernel,
        grid=(DEPTH, n_out // tn),
        in_specs=[
            pl.BlockSpec((N_MODS, D_MODEL), lambda l, j: (0, 0)),
            pl.BlockSpec((None, D_MODEL, tn), lambda l, j: (l, 0, j)),
            pl.BlockSpec((None, 1, tn), lambda l, j: (l, 0, j)),
        ],
        out_specs=pl.BlockSpec((None, N_MODS, tn), lambda l, j: (l, 0, j)),
        out_shape=jax.ShapeDtypeStruct((DEPTH, N_MODS, n_out), _F32),
        compiler_params=_cparams("arbitrary", "arbitrary"),
        name="ada_mods",
    )(cvec, ada_w, ada_b.reshape(DEPTH, 1, n_out))
    return out.reshape(DEPTH, N_MODS, 6, D_MODEL)


def _proj_kernel(*refs, rope):
    if rope:
        (x_ref, mod_ref, n1g_ref, win_ref, qg_ref, kg_ref, cqg_ref, wuq_ref, mqg_ref, ckvg_ref,
         cosa_ref, sina_ref, cosc_ref, sinc_ref,
         qa_ref, kab_ref, kaf_ref, vab_ref, vaf_ref, xb_ref, yb_ref, qc_ref, ckvn_ref, kr128_ref, kr_ref) = refs
    else:
        (x_ref, mod_ref, n1g_ref, win_ref, qg_ref, kg_ref, cqg_ref, wuq_ref, mqg_ref, ckvg_ref,
         qa_ref, kab_ref, kaf_ref, vab_ref, vaf_ref, xb_ref, yb_ref, qc_ref, ckvn_ref, kr128_ref, kr_ref) = refs
    x = x_ref[...]
    xn = x * lax.rsqrt(jnp.mean(x * x, axis=-1, keepdims=True) + EPS) * n1g_ref[...]
    h = xn * (1.0 + mod_ref[1:2, :]) + mod_ref[0:1, :]
    z = jnp.dot(h.astype(_BF), win_ref[...], preferred_element_type=_F32)

    for hd in range(A_HEADS):
        q = _head_rms(z[:, _QA0 + hd * LANES:_QA0 + (hd + 1) * LANES], A_HEAD_DIM) * qg_ref[...]
        if rope:
            q = _rope_rotate(q, cosa_ref[...], sina_ref[...])
        qa_ref[:, hd * LANES:(hd + 1) * LANES] = (q * (A_HEAD_DIM ** -0.5)).astype(_BF)
    k_heads = []
    for hd in range(A_KV_HEADS):
        k = _head_rms(z[:, _KA0 + hd * LANES:_KA0 + (hd + 1) * LANES], A_HEAD_DIM) * kg_ref[...]
        k_heads.append(k[:, :A_HEAD_DIM])
        if rope:
            k = _rope_rotate(k, cosa_ref[...], sina_ref[...])
        kab_ref[:, hd * LANES:(hd + 1) * LANES] = k.astype(_BF)
    kaf_ref[...] = jnp.concatenate(k_heads, axis=-1)
    va = z[:, _VA0:_VA0 + A_KV_HEADS * A_HEAD_DIM]
    vaf_ref[...] = va
    vab_ref[...] = va.astype(_BF)

    xb_ref[...] = z[:, _XB0:_XB0 + LRU_WIDTH]
    yb_ref[...] = z[:, _YB0:_YB0 + LRU_WIDTH]

    cq = z[:, _CQ0:_CQ0 + Q_LORA]
    cqn = cq * lax.rsqrt(jnp.mean(cq * cq, axis=-1, keepdims=True) + EPS) * cqg_ref[...]
    zq = jnp.dot(cqn.astype(_BF), wuq_ref[...], preferred_element_type=_F32)
    for hd in range(MLA_HEADS):
        q = _head_rms(zq[:, hd * LANES:(hd + 1) * LANES], MLA_QK) * mqg_ref[...]
        if rope:
            q = _rope_rotate(q, cosc_ref[...], sinc_ref[...])
        qc_ref[:, hd * LANES:(hd + 1) * LANES] = (q * (MLA_QK ** -0.5)).astype(_BF)
    ckv = z[:, _CKV0:_CKV0 + KV_LORA]
    ckvn_ref[...] = ckv * lax.rsqrt(jnp.mean(ckv * ckv, axis=-1, keepdims=True) + EPS) * ckvg_ref[...]
    kr128 = z[:, _KR0:_KR0 + LANES]
    kr128_ref[...] = kr128
    kr_ref[...] = kr128[:, MLA_NOPE:MLA_NOPE + MLA_ROPE]


def _proj(x, mods_l, wts, tabs, *, rope, tm, seq):
    t = x.shape[0]
    nt = t // tm
    per_seq = seq // tm

    def row(i):
        return (i, 0)

    def const2(i):
        return (0, 0)

    if rope:
        def mod_idx(i):
            return (1 + i // per_seq, 0, 0)

        def tab_idx(i):
            return (i % per_seq, 0)
    else:
        def mod_idx(i):
            return (0, 0, 0)

    in_specs = [
        pl.BlockSpec((tm, D_MODEL), row),
        pl.BlockSpec((None, 6, D_MODEL), mod_idx),
        pl.BlockSpec((1, D_MODEL), const2),
        pl.BlockSpec((D_MODEL, _NP), const2),
        pl.BlockSpec((1, LANES), const2),
        pl.BlockSpec((1, LANES), const2),
        pl.BlockSpec((1, Q_LORA), const2),
        pl.BlockSpec((Q_LORA, MLA_HEADS * LANES), const2),
        pl.BlockSpec((1, LANES), const2),
        pl.BlockSpec((1, KV_LORA), const2),
    ]
    args = [x, mods_l, wts["norm1_g"], wts["w_in"], wts["a_q_g"], wts["a_k_g"], wts["mla_cq_g"], wts["w_uq"],
            wts["mla_q_g"], wts["mla_ckv_g"]]
    if rope:
        in_specs += [pl.BlockSpec((tm, LANES), tab_idx)] * 4
        args += list(tabs)
    widths = [(A_HEADS * LANES, _BF), (A_KV_HEADS * LANES, _BF), (A_KV_HEADS * A_HEAD_DIM, _F32),
              (A_KV_HEADS * A_HEAD_DIM, _BF), (A_KV_HEADS * A_HEAD_DIM, _F32), (LRU_WIDTH, _F32), (LRU_WIDTH, _F32),
              (MLA_HEADS * LANES, _BF), (KV_LORA, _F32), (LANES, _F32), (MLA_ROPE, _F32)]
    out_specs = [pl.BlockSpec((tm, w), row) for w, _ in widths]
    out_shape = [jax.ShapeDtypeStruct((t, w), dt) for w, dt in widths]
    return pl.pallas_call(
        functools.partial(_proj_kernel, rope=rope),
        grid=(nt,),
        in_specs=in_specs,
        out_specs=out_specs,
        out_shape=out_shape,
        compiler_params=_cparams("arbitrary"),
        name="proj_rope" if rope else "proj",
    )(*args)


def _mlakv_kernel(*refs, rope):
    if rope:
        ckvn_ref, kr128_ref, wukv_ref, kg_ref, cos_ref, sin_ref, k_ref, v_ref = refs
    else:
        ckvn_ref, kr128_ref, wukv_ref, kg_ref, k_ref, v_ref = refs
    kv = jnp.dot(ckvn_ref[...].astype(_BF), wukv_ref[...], preferred_element_type=_F32)
    kr128 = kr128_ref[...]
    for hd in range(MLA_HEADS):
        k = _head_rms(kv[:, hd * LANES:(hd + 1) * LANES] + kr128, MLA_QK) * kg_ref[...]
        if rope:
            k = _rope_rotate(k, cos_ref[...], sin_ref[...])
        k_ref[:, hd * LANES:(hd + 1) * LANES] = k.astype(_BF)
    v_ref[...] = kv[:, MLA_HEADS * LANES:].astype(_BF)


def _mlakv(ckvn, kr128, wts, tabs, *, rope, tm, seq):
    t = ckvn.shape[0]
    per_seq = seq // tm
    in_specs = [
        pl.BlockSpec((tm, KV_LORA), lambda i: (i, 0)),
        pl.BlockSpec((tm, LANES), lambda i: (i, 0)),
        pl.BlockSpec((KV_LORA, MLA_HEADS * (LANES + MLA_V)), lambda i: (0, 0)),
        pl.BlockSpec((1, LANES), lambda i: (0, 0)),
    ]
    args = [ckvn, kr128, wts["w_ukv"], wts["mla_k_g"]]
    if rope:
        in_specs += [pl.BlockSpec((tm, LANES), lambda i: (i % per_seq, 0))] * 2
        args += [tabs[2], tabs[3]]
    return pl.pallas_call(
        functools.partial(_mlakv_kernel, rope=rope),
        grid=(t // tm,),
        in_specs=in_specs,
        out_specs=[pl.BlockSpec((tm, MLA_HEADS * LANES), lambda i: (i, 0)),
                   pl.BlockSpec((tm, MLA_HEADS * MLA_V), lambda i: (i, 0))],
        out_shape=[jax.ShapeDtypeStruct((t, MLA_HEADS * LANES), _BF),
                   jax.ShapeDtypeStruct((t, MLA_HEADS * MLA_V), _BF)],
        compiler_params=_cparams("arbitrary"),
        name="mlakv_rope" if rope else "mlakv",
    )(*args)


_CHUNK = 8


def _lru_kernel(xb_ref, yb_ref, h0_ref, cw_ref, cb_ref, wg_ref, bg_ref, lam_ref,
                ob_ref, st_ref, af_ref, bf_ref, ab_ref, bb_ref, h_ref, *, n):
    x = xb_ref[...]
    rows = lax.broadcasted_iota(jnp.int32, x.shape, 0)
    xc = cb_ref[...] + x * cw_ref[2:3, :]
    xc = xc + jnp.where(rows >= 2, pltpu.roll(x, 2, 0), 0.0) * cw_ref[0:1, :]
    xc = xc + jnp.where(rows >= 1, pltpu.roll(x, 1, 0), 0.0) * cw_ref[1:2, :]
    xc = xc + jnp.where(rows < n - 1, pltpu.roll(x, n - 1, 0), 0.0) * cw_ref[3:4, :]

    xcb = xc.astype(_BF)
    half = LRU_WIDTH // 2
    pre = [jnp.dot(xcb[:, j * half:(j + 1) * half], wg_ref[j], preferred_element_type=_F32) for j in range(2)]
    for d, (a_ref, b_ref) in enumerate(((af_ref, bf_ref), (ab_ref, bb_ref))):
        pa = jnp.concatenate([pre[0][:, (2 * d) * half:(2 * d + 1) * half],
                              pre[1][:, (2 * d) * half:(2 * d + 1) * half]], axis=-1)
        px = jnp.concatenate([pre[0][:, (2 * d + 1) * half:(2 * d + 2) * half],
                              pre[1][:, (2 * d + 1) * half:(2 * d + 2) * half]], axis=-1)
        r = jax.nn.sigmoid(pa + bg_ref[2 * d:2 * d + 1, :])
        i = jax.nn.sigmoid(px + bg_ref[2 * d + 1:2 * d + 2, :])
        nlam = -lam_ref[d:d + 1, :]
        softplus = jnp.maximum(nlam, 0.0) + jnp.log1p(jnp.exp(-jnp.abs(nlam)))
        log_a = (-LRU_C) * r * softplus
        a = jnp.exp(log_a)
        a_ref[...] = a
        b_ref[...] = jnp.sqrt(-jnp.tanh(log_a) * (a * a + 1.0)) * i * xc

    nchunks = n // _CHUNK
    crow = lax.broadcasted_iota(jnp.int32, (_CHUNK, LRU_WIDTH), 0)

    def chunk_scan(a, b, carry, reverse):
        for d in (1, 2, 4):
            if reverse:
                keep = crow < _CHUNK - d
                a_s = pltpu.roll(a, _CHUNK - d, 0)
                b_s = pltpu.roll(b, _CHUNK - d, 0)
            else:
                keep = crow >= d
                a_s = pltpu.roll(a, d, 0)
                b_s = pltpu.roll(b, d, 0)
            b = jnp.where(keep, a * b_s + b, b)
            a = jnp.where(keep, a * a_s, a)
        return a * carry + b

    def body(ci, carry):
        cf, cb = carry
        off_f = pl.multiple_of(ci * _CHUNK, _CHUNK)
        off_b = pl.multiple_of((nchunks - 1 - ci) * _CHUNK, _CHUNK)
        hf = chunk_scan(af_ref[pl.ds(off_f, _CHUNK), :], bf_ref[pl.ds(off_f, _CHUNK), :], cf, False)
        hb = chunk_scan(ab_ref[pl.ds(off_b, _CHUNK), :], bb_ref[pl.ds(off_b, _CHUNK), :], cb, True)
        bf_ref[pl.ds(off_f, _CHUNK), :] = hf
        bb_ref[pl.ds(off_b, _CHUNK), :] = hb
        return hf[_CHUNK - 1:_CHUNK, :], hb[0:1, :]

    cf, cb = lax.fori_loop(0, nchunks, body, (h0_ref[0:1, :], h0_ref[1:2, :]))
    st_ref[0:1, :] = cf
    st_ref[1:2, :] = cb
    ob_ref[...] = ((bf_ref[...] + bb_ref[...]) * jax.nn.gelu(yb_ref[...])).astype(_BF)


def _lru(xb, yb, h0, wts, *, n):
    nb = xb.shape[0] // n
    seq = lambda b: (b, 0)
    const2 = lambda b: (0, 0)
    return pl.pallas_call(
        functools.partial(_lru_kernel, n=n),
        grid=(nb,),
        in_specs=[
            pl.BlockSpec((n, LRU_WIDTH), seq),
            pl.BlockSpec((n, LRU_WIDTH), seq),
            pl.BlockSpec((None, 2, LRU_WIDTH), lambda b: (b, 0, 0)),
            pl.BlockSpec((CONV_W, LRU_WIDTH), const2),
            pl.BlockSpec((1, LRU_WIDTH), const2),
            pl.BlockSpec((2, LRU_WIDTH // 2, 2 * LRU_WIDTH), lambda b: (0, 0, 0)),
            pl.BlockSpec((4, LRU_WIDTH), const2),
            pl.BlockSpec((2, LRU_WIDTH), const2),
        ],
        out_specs=[pl.BlockSpec((n, LRU_WIDTH), seq),
                   pl.BlockSpec((None, 2, LRU_WIDTH), lambda b: (b, 0, 0))],
        out_shape=[jax.ShapeDtypeStruct((nb * n, LRU_WIDTH), _BF),
                   jax.ShapeDtypeStruct((nb, 2, LRU_WIDTH), _F32)],
        scratch_shapes=[pltpu.VMEM((n, LRU_WIDTH), _F32) for _ in range(5)],
        compiler_params=_cparams("arbitrary"),
        name=f"lru_{n}",
    )(xb, yb, h0, wts["conv_w"], wts["conv_b"], wts["lru_wg"], wts["lru_bg"], wts["lru_lam"])


def _attn_kernel(*refs, heads, kv_heads, dv, cached):
    if cached:
        q_ref, k_ref, v_ref, kc_ref, vc_ref, o_ref = refs
    else:
        q_ref, k_ref, v_ref, o_ref = refs
    group = heads // kv_heads
    contract_last = (((1,), (1,)), ((), ()))
    outs = []
    for hd in range(heads):
        g = hd // group
        q = q_ref[:, hd * LANES:(hd + 1) * LANES]
        s = lax.dot_general(q, k_ref[:, g * LANES:(g + 1) * LANES], contract_last, preferred_element_type=_F32)
        m = jnp.max(s, axis=-1, keepdims=True)
        if cached:
            sc = lax.dot_general(q, kc_ref[:, g * LANES:(g + 1) * LANES], contract_last,
                                 preferred_element_type=_F32)
            m = jnp.maximum(m, jnp.max(sc, axis=-1, keepdims=True))
        p = jnp.exp(s - m)
        den = jnp.sum(p, axis=-1, keepdims=True)
        o = jnp.dot(p.astype(_BF), v_ref[:, g * dv:(g + 1) * dv], preferred_element_type=_F32)
        if cached:
            pc = jnp.exp(sc - m)
            den = den + jnp.sum(pc, axis=-1, keepdims=True)
            o = o + jnp.dot(pc.astype(_BF), vc_ref[:, g * dv:(g + 1) * dv], preferred_element_type=_F32)
        outs.append(o / den)
    o_ref[...] = jnp.concatenate(outs, axis=-1).astype(_BF)


def _attn(q, k, v, kc, vc, *, heads, kv_heads, dv, n, tq, name):
    t = q.shape[0]
    nb = t // n
    nq = n // tq
    cached = kc is not None
    in_specs = [
        pl.BlockSpec((tq, heads * LANES), lambda b, i: (b * nq + i, 0)),
        pl.BlockSpec((n, kv_heads * LANES), lambda b, i: (b, 0)),
        pl.BlockSpec((n, kv_heads * dv), lambda b, i: (b, 0)),
    ]
    args = [q, k, v]
    if cached:
        nc = kc.shape[0] // nb
        in_specs += [pl.BlockSpec((nc, kv_heads * LANES), lambda b, i: (b, 0)),
                     pl.BlockSpec((nc, kv_heads * dv), lambda b, i: (b, 0))]
        args += [kc, vc]
    return pl.pallas_call(
        functools.partial(_attn_kernel, heads=heads, kv_heads=kv_heads, dv=dv, cached=cached),
        grid=(nb, nq),
        in_specs=in_specs,
        out_specs=pl.BlockSpec((tq, heads * dv), lambda b, i: (b * nq + i, 0)),
        out_shape=jax.ShapeDtypeStruct((t, heads * dv), _BF),
        compiler_params=_cparams("arbitrary", "arbitrary"),
        name=name,
    )(*args)


def _post_kernel(oa_ref, ob_ref, oc_ref, x_ref, mod_ref, wout_ref, n2g_ref, rw_ref, rb_ref,
                 x1_ref, h2_ref, tidx_ref, tw_ref):
    w0 = A_HEADS * A_HEAD_DIM
    w1 = w0 + LRU_WIDTH
    o = jnp.dot(oa_ref[...], wout_ref[0:w0, :], preferred_element_type=_F32)
    o = o + jnp.dot(ob_ref[...], wout_ref[w0:w1, :], preferred_element_type=_F32)
    o = o + jnp.dot(oc_ref[...], wout_ref[w1:, :], preferred_element_type=_F32)
    x1 = x_ref[...] + mod_ref[2:3, :] * o
    x1_ref[...] = x1
    xn = x1 * lax.rsqrt(jnp.mean(x1 * x1, axis=-1, keepdims=True) + EPS) * n2g_ref[...]
    h2 = xn * (1.0 + mod_ref[4:5, :]) + mod_ref[3:4, :]
    h2_ref[...] = h2

    logits = _split_dot(h2, rw_ref[...]) + rb_ref[...]
    lane = lax.broadcasted_iota(jnp.int32, logits.shape, 1).astype(_F32)
    neg = jnp.float32(-jnp.inf)
    work = logits
    top = None
    tidx = jnp.zeros(logits.shape, _F32)
    tw = jnp.zeros(logits.shape, _F32)
    for k in range(TOP_K):
        m = jnp.max(work, axis=-1, keepdims=True)
        if top is None:
            top = m
        first = jnp.min(jnp.where(work == m, lane, float(LANES)), axis=-1, keepdims=True)
        work = jnp.where(lane == first, neg, work)
        tidx = jnp.where(lane == float(k), first, tidx)
        tw = jnp.where(lane == float(k), jnp.exp(m - top), tw)
    tidx_ref[...] = tidx.astype(jnp.int32)
    tw_ref[...] = tw / jnp.sum(tw, axis=-1, keepdims=True)


def _post(oa, ob, oc, x, mods_l, wts, *, rope, tm, seq):
    t = x.shape[0]
    per_seq = seq // tm
    row = lambda i: (i, 0)
    const2 = lambda i: (0, 0)
    mod_idx = (lambda i: (1 + i // per_seq, 0, 0)) if rope else (lambda i: (0, 0, 0))
    mix = A_HEADS * A_HEAD_DIM + LRU_WIDTH + MLA_HEADS * MLA_V
    return pl.pallas_call(
        _post_kernel,
        grid=(t // tm,),
        in_specs=[
            pl.BlockSpec((tm, A_HEADS * A_HEAD_DIM), row),
            pl.BlockSpec((tm, LRU_WIDTH), row),
            pl.BlockSpec((tm, MLA_HEADS * MLA_V), row),
            pl.BlockSpec((tm, D_MODEL), row),
            pl.BlockSpec((None, 6, D_MODEL), mod_idx),
            pl.BlockSpec((mix, D_MODEL), const2),
            pl.BlockSpec((1, D_MODEL), const2),
            pl.BlockSpec((D_MODEL, LANES), const2),
            pl.BlockSpec((1, LANES), const2),
        ],
        out_specs=[pl.BlockSpec((tm, D_MODEL), row), pl.BlockSpec((tm, D_MODEL), row),
                   pl.BlockSpec((tm, LANES), row), pl.BlockSpec((tm, LANES), row)],
        out_shape=[jax.ShapeDtypeStruct((t, D_MODEL), _F32), jax.ShapeDtypeStruct((t, D_MODEL), _F32),
                   jax.ShapeDtypeStruct((t, LANES), jnp.int32), jax.ShapeDtypeStruct((t, LANES), _F32)],
        compiler_params=_cparams("arbitrary"),
        name="post",
    )(oa, ob, oc, x, mods_l, wts["w_out"], wts["norm2_g"], wts["router_w"], wts["router_b"])


MOE_TM = 256
_K_SHIFT = TOP_K.bit_length() - 1
assert 1 << _K_SHIFT == TOP_K


def _route(tidx, t):
    a = t * TOP_K
    nts = a // MOE_TM + N_EXPERTS
    flat_e = tidx[:, :TOP_K].reshape(a)
    key = jnp.sort(flat_e * a + jnp.arange(a, dtype=jnp.int32))
    order = key % a
    sorted_e = key // a
    cnt = jnp.sum((sorted_e[:, None] == jnp.arange(N_EXPERTS, dtype=jnp.int32)[None, :]).astype(jnp.int32), axis=0)
    tiles = (cnt + MOE_TM - 1) // MOE_TM
    tile_end = jnp.cumsum(tiles)
    tile_start = tile_end - tiles
    grp_start = jnp.cumsum(cnt) - cnt
    n_tiles = tile_end[-1]
    tile_id = jnp.arange(nts, dtype=jnp.int32)
    tile_e = jnp.minimum(jnp.sum((tile_id[:, None] >= tile_end[None, :]).astype(jnp.int32), axis=1), N_EXPERTS - 1)
    tile_e = jnp.where(tile_id < n_tiles, tile_e, tile_e[jnp.maximum(n_tiles - 1, 0)])
    rank = (tile_id - tile_start[tile_e])[:, None] * MOE_TM + jnp.arange(MOE_TM, dtype=jnp.int32)[None, :]
    valid = (rank < cnt[tile_e][:, None]) & (tile_id < n_tiles)[:, None]
    j = jnp.clip(grp_start[tile_e][:, None] + rank, 0, a - 1)
    code = jnp.where(valid, order[j], -1).reshape(nts * MOE_TM)
    tile_rows = jnp.sum(valid.astype(jnp.int32), axis=1)
    return (tile_e.astype(jnp.int32), tile_rows.astype(jnp.int32), n_tiles.reshape(1).astype(jnp.int32),
            code.astype(jnp.int32))


def _ffn_kernel(te_ref, rows_ref, nt_ref, code_ref, h2_hbm, w1_ref, b1_ref, w2_ref, b2_ref, yk_hbm,
                xbuf, ybuf, w1b, w2b, gsem, ssem, *, t):
    i = pl.program_id(0)
    nts = pl.num_programs(0)
    nt = nt_ref[0]
    slot = i % 2

    def gather_copy(tok, r, s):
        return pltpu.make_async_copy(h2_hbm.at[pl.ds(tok, 1)], xbuf.at[s, pl.ds(r, 1)], gsem.at[s])

    def scatter_copy(dst, r, s):
        return pltpu.make_async_copy(ybuf.at[s, pl.ds(r, 1)], yk_hbm.at[pl.ds(dst, 1)], ssem.at[s])

    def issue_gather(tile, s):
        base = tile * MOE_TM

        def body(r, carry):
            code = code_ref[base + r]
            gather_copy(lax.shift_right_logical(jnp.maximum(code, 0), _K_SHIFT), r, s).start()
            return carry

        lax.fori_loop(0, MOE_TM, body, 0, unroll=8)

    def wait_gather(s):
        pltpu.make_async_copy(h2_hbm.at[pl.ds(0, MOE_TM)], xbuf.at[s], gsem.at[s]).wait()

    def wait_scatter(s, n_rows):
        size = MOE_TM
        while size >= 1:
            @pl.when((n_rows & size) != 0)
            def _(size=size):
                pltpu.make_async_copy(ybuf.at[s, pl.ds(0, size)], yk_hbm.at[pl.ds(0, size)], ssem.at[s]).wait()

            size //= 2

    @pl.when(i == 0)
    def _():
        issue_gather(0, 0)

    @pl.when(i + 1 < nt)
    def _():
        issue_gather(i + 1, 1 - slot)

    @pl.when(i < nt)
    def _():
        e_changed = jnp.logical_or(i == 0, te_ref[i] != te_ref[jnp.maximum(i - 1, 0)])

        @pl.when(e_changed)
        def _():
            w1b[...] = w1_ref[...].astype(_BF)
            w2b[...] = w2_ref[...].astype(_BF)

        wait_gather(slot)
        x = xbuf[slot].astype(_BF)
        gu = jnp.dot(x, w1b[...], preferred_element_type=_F32) + b1_ref[...]
        gt = jnp.minimum(gu[:, :D_FF], SWIGLU_LIMIT)
        up = jnp.clip(gu[:, D_FF:], -SWIGLU_LIMIT, SWIGLU_LIMIT)
        act = gt * jax.nn.sigmoid(SWIGLU_ALPHA * gt) * (up + 1.0)
        y = jnp.dot(act.astype(_BF), w2b[...], preferred_element_type=_F32) + b2_ref[...]

        @pl.when(i >= 2)
        def _():
            wait_scatter(slot, rows_ref[jnp.maximum(i - 2, 0)])

        ybuf[slot] = y
        base = i * MOE_TM
        n_rows = rows_ref[i]

        def body(r, carry):
            code = code_ref[base + r]
            dst = (code & (TOP_K - 1)) * t + lax.shift_right_logical(code, _K_SHIFT)
            scatter_copy(dst, r, slot).start()
            return carry

        @pl.when(n_rows == MOE_TM)
        def _():
            lax.fori_loop(0, MOE_TM, body, 0, unroll=8)

        @pl.when(n_rows < MOE_TM)
        def _():
            lax.fori_loop(0, n_rows, body, 0)

    @pl.when(i == nts - 1)
    def _():
        @pl.when(nt >= 1)
        def _():
            wait_scatter((nt - 1) % 2, rows_ref[jnp.maximum(nt - 1, 0)])

        @pl.when(nt >= 2)
        def _():
            wait_scatter(nt % 2, rows_ref[jnp.maximum(nt - 2, 0)])


def _ffn(h2, tile_e, tile_rows, n_tiles, code, moe_w1, moe_w2, wts, layer):
    t = h2.shape[0]
    nts = tile_e.shape[0]
    by_expert = lambda i, te, rows, nt, code: (te[i], 0, 0)
    by_layer_expert = lambda i, te, rows, nt, code: (layer, te[i], 0, 0)
    grid_spec = pltpu.PrefetchScalarGridSpec(
        num_scalar_prefetch=4,
        grid=(nts,),
        in_specs=[
            pl.BlockSpec(memory_space=pl.ANY),
            pl.BlockSpec((None, None, D_MODEL, 2 * D_FF), by_layer_expert),
            pl.BlockSpec((None, 1, 2 * D_FF), by_expert),
            pl.BlockSpec((None, None, D_FF, D_MODEL), by_layer_expert),
            pl.BlockSpec((None, 1, D_MODEL), by_expert),
        ],
        out_specs=pl.BlockSpec(memory_space=pl.ANY),
        scratch_shapes=[
            pltpu.VMEM((2, MOE_TM, D_MODEL), _F32),
            pltpu.VMEM((2, MOE_TM, D_MODEL), _F32),
            pltpu.VMEM((D_MODEL, 2 * D_FF), _BF),
            pltpu.VMEM((D_FF, D_MODEL), _BF),
            pltpu.SemaphoreType.DMA((2,)),
            pltpu.SemaphoreType.DMA((2,)),
        ],
    )
    return pl.pallas_call(
        functools.partial(_ffn_kernel, t=t),
        grid_spec=grid_spec,
        out_shape=jax.ShapeDtypeStruct((TOP_K * t, D_MODEL), _F32),
        compiler_params=_cparams("arbitrary"),
        name="moe_ffn",
    )(tile_e, tile_rows, n_tiles, code, h2, moe_w1, wts["moe_b1"], moe_w2, wts["moe_b2"])


def _combine_kernel(x1_ref, tw_ref, mod_ref, y0_ref, y1_ref, y2_ref, y3_ref, o_ref):
    tw = tw_ref[...]
    acc = tw[:, 0:1] * y0_ref[...]
    acc = acc + tw[:, 1:2] * y1_ref[...]
    acc = acc + tw[:, 2:3] * y2_ref[...]
    acc = acc + tw[:, 3:4] * y3_ref[...]
    o_ref[...] = x1_ref[...] + mod_ref[5:6, :] * acc


def _combine(x1, tw, mods_l, yk, *, tm, n_ctx, seq):
    t = x1.shape[0]
    nt = t // tm
    ctx_tiles = n_ctx // tm
    per_seq = seq // tm
    row = lambda i: (i, 0)
    mod_idx = lambda i: (jnp.where(i < ctx_tiles, 0, 1 + (i - ctx_tiles) // per_seq), 0, 0)
    plane = lambda k: (lambda i: (k * nt + i, 0))
    return pl.pallas_call(
        _combine_kernel,
        grid=(nt,),
        in_specs=[pl.BlockSpec((tm, D_MODEL), row), pl.BlockSpec((tm, LANES), row),
                  pl.BlockSpec((None, 6, D_MODEL), mod_idx)]
                 + [pl.BlockSpec((tm, D_MODEL), plane(k)) for k in range(TOP_K)],
        out_specs=pl.BlockSpec((tm, D_MODEL), row),
        out_shape=jax.ShapeDtypeStruct((t, D_MODEL), _F32),
        compiler_params=_cparams("arbitrary"),
        name="moe_combine",
    )(x1, tw, mods_l, yk, yk, yk, yk)


def _pad_heads(w, heads, dim):
    lead = w.shape[:-1]
    w = w.reshape(lead + (heads, dim))
    w = jnp.pad(w, [(0, 0)] * len(lead) + [(0, 0), (0, LANES - dim)])
    return w.reshape(lead + (heads * LANES,))


def _pad_lanes(g, left=0):
    return jnp.pad(g, (left, LANES - left - g.shape[0])).reshape(1, LANES)


def _block_diag_halves(w):
    w4 = w.reshape(2, 4, 64, 64)
    eye = jnp.eye(4, dtype=w.dtype)
    return jnp.einsum("jaik,ab->jaibk", w4, eye).reshape(2, 256, 256)


def _layer_weights(l, p):
    w_in = p["w_in"][l]
    qa, ka, va, xb, yb, cq, ckv, kr = jnp.split(w_in, (512, 640, 768, 1280, 1792, 2176, 2432), axis=-1)
    kr128 = jnp.pad(kr, ((0, 0), (MLA_NOPE, LANES - MLA_NOPE - MLA_ROPE)))
    w_in_p = jnp.concatenate([_pad_heads(qa, A_HEADS, A_HEAD_DIM), _pad_heads(ka, A_KV_HEADS, A_HEAD_DIM), va, xb,
                              yb, cq, ckv, kr128], axis=-1).astype(_BF)
    ukv = p["mla_w_ukv"][l].reshape(KV_LORA, MLA_HEADS, MLA_NOPE + MLA_V)
    w_ukv = jnp.concatenate([_pad_heads(ukv[:, :, :MLA_NOPE].reshape(KV_LORA, -1), MLA_HEADS, MLA_NOPE),
                             ukv[:, :, MLA_NOPE:].reshape(KV_LORA, -1)], axis=-1).astype(_BF)
    wg = jnp.concatenate([_block_diag_halves(p["lru_wa"][l, 0]), _block_diag_halves(p["lru_wx"][l, 0]),
                          _block_diag_halves(p["lru_wa"][l, 1]), _block_diag_halves(p["lru_wx"][l, 1])],
                         axis=-1).astype(_BF)
    bg = jnp.stack([p["lru_ba"][l, 0], p["lru_bx"][l, 0], p["lru_ba"][l, 1], p["lru_bx"][l, 1]])
    return {
        "norm1_g": p["norm1_g"][l].reshape(1, -1),
        "norm2_g": p["norm2_g"][l].reshape(1, -1),
        "w_in": w_in_p,
        "a_q_g": _pad_lanes(p["a_q_g"][l]),
        "a_k_g": _pad_lanes(p["a_k_g"][l]),
        "mla_cq_g": p["mla_cq_g"][l].reshape(1, -1),
        "w_uq": _pad_heads(p["mla_w_uq"][l], MLA_HEADS, MLA_QK).astype(_BF),
        "mla_q_g": _pad_lanes(p["mla_q_g"][l]),
        "mla_ckv_g": p["mla_ckv_g"][l].reshape(1, -1),
        "w_ukv": w_ukv,
        "mla_k_g": _pad_lanes(p["mla_k_g"][l]),
        "conv_w": p["conv_w"][l],
        "conv_b": p["conv_b"][l].reshape(1, -1),
        "lru_wg": wg,
        "lru_bg": bg,
        "lru_lam": p["lru_lam"][l],
        "w_out": p["w_out"][l].astype(_BF),
        "router_w": jnp.pad(p["router_w"][l], ((0, 0), (0, LANES - N_EXPERTS))),
        "router_b": jnp.pad(p["router_b"][l], (0, LANES - N_EXPERTS), constant_values=-1e30).reshape(1, LANES),
        "moe_b1": p["moe_b1"][l].reshape(N_EXPERTS, 1, 2 * D_FF),
        "moe_b2": p["moe_b2"][l].reshape(N_EXPERTS, 1, D_MODEL),
    }


def _rope_tables(n, rot_dim, lane0):
    rows = n // GRID_W
    r = jnp.repeat(jnp.arange(rows, dtype=_F32), GRID_W)
    col = jnp.tile(jnp.arange(GRID_W, dtype=_F32), rows)
    n_freq = rot_dim // 4
    inv = ROPE_THETA ** (-jnp.arange(n_freq, dtype=_F32) / n_freq)
    ang = jnp.concatenate([r[:, None] * inv, col[:, None] * inv], axis=-1)
    cos = jnp.repeat(jnp.cos(ang), 2, axis=-1)
    sin = jnp.repeat(jnp.sin(ang), 2, axis=-1) * jnp.tile(jnp.array([-1.0, 1.0], _F32), rot_dim // 2)
    pad = ((0, 0), (lane0, LANES - lane0 - rot_dim))
    return jnp.pad(cos, pad, constant_values=1.0), jnp.pad(sin, pad)


def kernel(x_prompt, x_sample, cache_attn_k, cache_attn_v, state_lru, cache_mla_ckv, cache_mla_krope, c, c_ctx,
           ada_w, ada_b, norm1_g, norm2_g, w_in, a_q_g, a_k_g, conv_w, conv_b, lru_wa, lru_ba, lru_wx, lru_bx,
           lru_lam, mla_cq_g, mla_w_uq, mla_q_g, mla_ckv_g, mla_w_ukv, mla_k_g, w_out, router_w, router_b,
           moe_w1, moe_b1, moe_w2, moe_b2):
    p = dict(norm1_g=norm1_g, norm2_g=norm2_g, w_in=w_in, a_q_g=a_q_g, a_k_g=a_k_g, conv_w=conv_w, conv_b=conv_b,
             lru_wa=lru_wa, lru_ba=lru_ba, lru_wx=lru_wx, lru_bx=lru_bx, lru_lam=lru_lam, mla_cq_g=mla_cq_g,
             mla_w_uq=mla_w_uq, mla_q_g=mla_q_g, mla_ckv_g=mla_ckv_g, mla_w_ukv=mla_w_ukv, mla_k_g=mla_k_g,
             w_out=w_out, router_w=router_w, router_b=router_b, moe_w1=moe_w1, moe_b1=moe_b1, moe_w2=moe_w2,
             moe_b2=moe_b2)
    batch, seq, _ = x_prompt.shape
    dec_batch, dec_seq, _ = x_sample.shape
    past = cache_attn_k.shape[2]

    cvec = jnp.concatenate([c_ctx[None, :], c, jnp.zeros((N_MODS - 1 - dec_batch, D_MODEL), _F32)], axis=0)
    mods = _ada_mods(cvec, ada_w, ada_b)
    cos_a, sin_a = _rope_tables(dec_seq, A_HEAD_DIM, 0)
    cos_c, sin_c = _rope_tables(dec_seq, MLA_ROPE, MLA_NOPE)
    tabs = (cos_a, sin_a, cos_c, sin_c)

    yp = x_prompt.reshape(batch * seq, D_MODEL)
    ys = x_sample.reshape(dec_batch * dec_seq, D_MODEL)
    zero_state = jnp.zeros((batch, 2, LRU_WIDTH), _F32)
    ks_l, vs_l, hs_l, ckv_l, kr_l = [], [], [], [], []
    tm = 256
    n_tok = batch * seq + dec_batch * dec_seq
    for l in range(DEPTH):
        wts = _layer_weights(l, p)
        mods_l = mods[l]

        qa, kab, kaf, vab, vaf, xb, yb, qc, ckvn, kr128, kr = _proj(yp, mods_l, wts, None, rope=False, tm=tm, seq=seq)
        kc, vc = _mlakv(ckvn, kr128, wts, None, rope=False, tm=tm, seq=seq)
        ob, st = _lru(xb, yb, zero_state, wts, n=seq)
        oa = _attn(qa, kab, vab, None, None, heads=A_HEADS, kv_heads=A_KV_HEADS, dv=A_HEAD_DIM, n=seq, tq=seq,
                   name="attn_a_ctx")
        oc = _attn(qc, kc, vc, None, None, heads=MLA_HEADS, kv_heads=MLA_HEADS, dv=MLA_V, n=seq, tq=seq,
                   name="attn_c_ctx")
        post_ctx = _post(oa, ob, oc, yp, mods_l, wts, rope=False, tm=tm, seq=seq)
        ks_l.append(kaf.reshape(batch, seq, A_KV_HEADS, A_HEAD_DIM))
        vs_l.append(vaf.reshape(batch, seq, A_KV_HEADS, A_HEAD_DIM))
        hs_l.append(st)
        ckv_l.append(ckvn.reshape(batch, seq, KV_LORA))
        kr_l.append(kr.reshape(batch, seq, MLA_ROPE))

        qa, kab, _, vab, _, xb, yb, qc, ckvn, kr128, _ = _proj(ys, mods_l, wts, tabs, rope=True, tm=tm, seq=dec_seq)
        kc, vc = _mlakv(ckvn, kr128, wts, tabs, rope=True, tm=tm, seq=dec_seq)
        c_kr128 = jnp.pad(cache_mla_krope[:, l].reshape(dec_batch * past, MLA_ROPE),
                          ((0, 0), (MLA_NOPE, LANES - MLA_NOPE - MLA_ROPE)))
        kx, vx = _mlakv(cache_mla_ckv[:, l].reshape(dec_batch * past, KV_LORA), c_kr128, wts, None, rope=False,
                        tm=tm, seq=past)
        ob, _ = _lru(xb, yb, state_lru[:, l], wts, n=dec_seq)
        c_k = _pad_heads(cache_attn_k[:, l].reshape(dec_batch * past, A_KV_HEADS * A_HEAD_DIM), A_KV_HEADS,
                         A_HEAD_DIM).astype(_BF)
        c_v = cache_attn_v[:, l].reshape(dec_batch * past, A_KV_HEADS * A_HEAD_DIM).astype(_BF)
        oa = _attn(qa, kab, vab, c_k, c_v, heads=A_HEADS, kv_heads=A_KV_HEADS, dv=A_HEAD_DIM, n=dec_seq, tq=256,
                   name="attn_a_lat")
        oc = _attn(qc, kc, vc, kx, vx, heads=MLA_HEADS, kv_heads=MLA_HEADS, dv=MLA_V, n=dec_seq, tq=256,
                   name="attn_c_lat")
        post_lat = _post(oa, ob, oc, ys, mods_l, wts, rope=True, tm=tm, seq=dec_seq)

        x1, h2, tidx, tw = (jnp.concatenate([a, b], axis=0) for a, b in zip(post_ctx, post_lat))
        tile_e, tile_rows, n_tiles, code = _route(tidx, n_tok)
        yk = _ffn(h2, tile_e, tile_rows, n_tiles, code, moe_w1, moe_w2, wts, l)
        x2 = _combine(x1, tw, mods_l, yk, tm=tm, n_ctx=batch * seq, seq=dec_seq)
        yp, ys = x2[:batch * seq], x2[batch * seq:]

    return (yp.reshape(batch, seq, D_MODEL), ys.reshape(dec_batch, dec_seq, D_MODEL),
            jnp.stack(ks_l, axis=1), jnp.stack(vs_l, axis=1), jnp.stack(hs_l, axis=1),
            jnp.stack(ckv_l, axis=1), jnp.stack(kr_l, axis=1))
```

```python
import functools

import jax
import jax.numpy as jnp
from jax import lax
from jax.experimental import pallas as pl
from jax.experimental.pallas import tpu as pltpu

D_MODEL = 1024
DEPTH = 2
GRID_W = 64
ROPE_THETA = 10000.0
EPS = 1e-6
A_HEADS = 8
A_KV_HEADS = 2
A_HEAD_DIM = 64
LRU_WIDTH = 512
LRU_BLOCKS = 8
CONV_W = 4
LRU_C = 8.0
MLA_HEADS = 8
MLA_NOPE = 64
MLA_ROPE = 32
MLA_QK = MLA_NOPE + MLA_ROPE
MLA_V = 64
Q_LORA = 384
KV_LORA = 256
N_EXPERTS = 32
TOP_K = 4
D_FF = 1024
SWIGLU_LIMIT = 7.0
SWIGLU_ALPHA = 1.702

LANES = 128
N_MODS = 16
VMEM_LIMIT = 56 * 1024 * 1024

_QA0 = 0
_KA0 = _QA0 + A_HEADS * LANES
_VA0 = _KA0 + A_KV_HEADS * LANES
_XB0 = _VA0 + A_KV_HEADS * A_HEAD_DIM
_YB0 = _XB0 + LRU_WIDTH
_CQ0 = _YB0 + LRU_WIDTH
_CKV0 = _CQ0 + Q_LORA
_KR0 = _CKV0 + KV_LORA
_NP = _KR0 + LANES

_BF = jnp.bfloat16
_F32 = jnp.float32


def _cparams(*sem):
    return pltpu.CompilerParams(dimension_semantics=sem, vmem_limit_bytes=VMEM_LIMIT)


def _split_dot(a, b):
    a_hi = a.astype(_BF)
    a_lo = (a - a_hi.astype(_F32)).astype(_BF)
    b_hi = b.astype(_BF)
    b_lo = (b - b_hi.astype(_F32)).astype(_BF)
    d = functools.partial(jnp.dot, preferred_element_type=_F32)
    return d(a_hi, b_hi) + (d(a_hi, b_lo) + d(a_lo, b_hi))


def _rope_rotate(x, cos, sin_signed):
    lane = lax.broadcasted_iota(jnp.int32, x.shape, 1)
    swapped = jnp.where((lane & 1) == 0, pltpu.roll(x, LANES - 1, 1), pltpu.roll(x, 1, 1))
    return x * cos + swapped * sin_signed


def _head_rms(xh, denom):
    return xh * lax.rsqrt(jnp.sum(xh * xh, axis=-1, keepdims=True) * (1.0 / denom) + EPS)


def _mods_kernel(c_ref, w_ref, b_ref, o_ref):
    c = c_ref[...]
    s = c * jax.nn.sigmoid(c)
    o_ref[...] = _split_dot(s, w_ref[...]) + b_ref[...]


def _ada_mods(cvec, ada_w, ada_b):
    tn = 1536
    n_out = ada_w.shape[-1]
    out = pl.pallas_call(
        _mods_kernel,
        grid=(DEPTH, n_out // tn),
        in_specs=[
            pl.BlockSpec((N_MODS, D_MODEL), lambda l, j: (0, 0)),
            pl.BlockSpec((None, D_MODEL, tn), lambda l, j: (l, 0, j)),
            pl.BlockSpec((None, 1, tn), lambda l, j: (l, 0, j)),
        ],
        out_specs=pl.BlockSpec((None, N_MODS, tn), lambda l, j: (l, 0, j)),
        out_shape=jax.ShapeDtypeStruct((DEPTH, N_MODS, n_out), _F32),
        compiler_params=_cparams("arbitrary", "arbitrary"),
        name="ada_mods",
    )(cvec, ada_w, ada_b.reshape(DEPTH, 1, n_out))
    return out.reshape(DEPTH, N_MODS, 6, D_MODEL)


def _proj_kernel(*refs, rope):
    if rope:
        (x_ref, mod_ref, n1g_ref, win_ref, qg_ref, kg_ref, cqg_ref, wuq_ref, mqg_ref, ckvg_ref,
         cosa_ref, sina_ref, cosc_ref, sinc_ref,
         qa_ref, kab_ref, kaf_ref, vab_ref, vaf_ref, xb_ref, yb_ref, qc_ref, ckvn_ref, kr128_ref, kr_ref) = refs
    else:
        (x_ref, mod_ref, n1g_ref, win_ref, qg_ref, kg_ref, cqg_ref, wuq_ref, mqg_ref, ckvg_ref,
         qa_ref, kab_ref, kaf_ref, vab_ref, vaf_ref, xb_ref, yb_ref, qc_ref, ckvn_ref, kr128_ref, kr_ref) = refs
    x = x_ref[...]
    xn = x * lax.rsqrt(jnp.mean(x * x, axis=-1, keepdims=True) + EPS) * n1g_ref[...]
    h = xn * (1.0 + mod_ref[1:2, :]) + mod_ref[0:1, :]
    z = jnp.dot(h.astype(_BF), win_ref[...], preferred_element_type=_F32)

    for hd in range(A_HEADS):
        q = _head_rms(z[:, _QA0 + hd * LANES:_QA0 + (hd + 1) * LANES], A_HEAD_DIM) * qg_ref[...]
        if rope:
            q = _rope_rotate(q, cosa_ref[...], sina_ref[...])
        qa_ref[:, hd * LANES:(hd + 1) * LANES] = (q * (A_HEAD_DIM ** -0.5)).astype(_BF)
    k_heads = []
    for hd in range(A_KV_HEADS):
        k = _head_rms(z[:, _KA0 + hd * LANES:_KA0 + (hd + 1) * LANES], A_HEAD_DIM) * kg_ref[...]
        k_heads.append(k[:, :A_HEAD_DIM])
        if rope:
            k = _rope_rotate(k, cosa_ref[...], sina_ref[...])
        kab_ref[:, hd * LANES:(hd + 1) * LANES] = k.astype(_BF)
    kaf_ref[...] = jnp.concatenate(k_heads, axis=-1)
    va = z[:, _VA0:_VA0 + A_KV_HEADS * A_HEAD_DIM]
    vaf_ref[...] = va
    vab_ref[...] = va.astype(_BF)

    xb_ref[...] = z[:, _XB0:_XB0 + LRU_WIDTH]
    yb_ref[...] = z[:, _YB0:_YB0 + LRU_WIDTH]

    cq = z[:, _CQ0:_CQ0 + Q_LORA]
    cqn = cq * lax.rsqrt(jnp.mean(cq * cq, axis=-1, keepdims=True) + EPS) * cqg_ref[...]
    zq = jnp.dot(cqn.astype(_BF), wuq_ref[...], preferred_element_type=_F32)
    for hd in range(MLA_HEADS):
        q = _head_rms(zq[:, hd * LANES:(hd + 1) * LANES], MLA_QK) * mqg_ref[...]
        if rope:
            q = _rope_rotate(q, cosc_ref[...], sinc_ref[...])
        qc_ref[:, hd * LANES:(hd + 1) * LANES] = (q * (MLA_QK ** -0.5)).astype(_BF)
    ckv = z[:, _CKV0:_CKV0 + KV_LORA]
    ckvn_ref[...] = ckv * lax.rsqrt(jnp.mean(ckv * ckv, axis=-1, keepdims=True) + EPS) * ckvg_ref[...]
    kr128 = z[:, _KR0:_KR0 + LANES]
    kr128_ref[...] = kr128
    kr_ref[...] = kr128[:, MLA_NOPE:MLA_NOPE + MLA_ROPE]


def _proj(x, mods_l, wts, tabs, *, rope, tm, seq):
    t = x.shape[0]
    nt = t // tm
    per_seq = seq // tm

    def row(i):
        return (i, 0)

    def const2(i):
        return (0, 0)

    if rope:
        def mod_idx(i):
            return (1 + i // per_seq, 0, 0)

        def tab_idx(i):
            return (i % per_seq, 0)
    else:
        def mod_idx(i):
            return (0, 0, 0)

    in_specs = [
        pl.BlockSpec((tm, D_MODEL), row),
        pl.BlockSpec((None, 6, D_MODEL), mod_idx),
        pl.BlockSpec((1, D_MODEL), const2),
        pl.BlockSpec((D_MODEL, _NP), const2),
        pl.BlockSpec((1, LANES), const2),
        pl.BlockSpec((1, LANES), const2),
        pl.BlockSpec((1, Q_LORA), const2),
        pl.BlockSpec((Q_LORA, MLA_HEADS * LANES), const2),
        pl.BlockSpec((1, LANES), const2),
        pl.BlockSpec((1, KV_LORA), const2),
    ]
    args = [x, mods_l, wts["norm1_g"], wts["w_in"], wts["a_q_g"], wts["a_k_g"], wts["mla_cq_g"], wts["w_uq"],
            wts["mla_q_g"], wts["mla_ckv_g"]]
    if rope:
        in_specs += [pl.BlockSpec((tm, LANES), tab_idx)] * 4
        args += list(tabs)
    widths = [(A_HEADS * LANES, _BF), (A_KV_HEADS * LANES, _BF), (A_KV_HEADS * A_HEAD_DIM, _F32),
              (A_KV_HEADS * A_HEAD_DIM, _BF), (A_KV_HEADS * A_HEAD_DIM, _F32), (LRU_WIDTH, _F32), (LRU_WIDTH, _F32),
              (MLA_HEADS * LANES, _BF), (KV_LORA, _F32), (LANES, _F32), (MLA_ROPE, _F32)]
    out_specs = [pl.BlockSpec((tm, w), row) for w, _ in widths]
    out_shape = [jax.ShapeDtypeStruct((t, w), dt) for w, dt in widths]
    return pl.pallas_call(
        functools.partial(_proj_kernel, rope=rope),
        grid=(nt,),
        in_specs=in_specs,
        out_specs=out_specs,
        out_shape=out_shape,
        compiler_params=_cparams("arbitrary"),
        name="proj_rope" if rope else "proj",
    )(*args)


def _mlakv_kernel(*refs, rope):
    if rope:
        ckvn_ref, kr128_ref, wukv_ref, kg_ref, cos_ref, sin_ref, k_ref, v_ref = refs
    else:
        ckvn_ref, kr128_ref, wukv_ref, kg_ref, k_ref, v_ref = refs
    kv = jnp.dot(ckvn_ref[...].astype(_BF), wukv_ref[...], preferred_element_type=_F32)
    kr128 = kr128_ref[...]
    for hd in range(MLA_HEADS):
        k = _head_rms(kv[:, hd * LANES:(hd + 1) * LANES] + kr128, MLA_QK) * kg_ref[...]
        if rope:
            k = _rope_rotate(k, cos_ref[...], sin_ref[...])
        k_ref[:, hd * LANES:(hd + 1) * LANES] = k.astype(_BF)
    v_ref[...] = kv[:, MLA_HEADS * LANES:].astype(_BF)


def _mlakv(ckvn, kr128, wts, tabs, *, rope, tm, seq):
    t = ckvn.shape[0]
    per_seq = seq // tm
    in_specs = [
        pl.BlockSpec((tm, KV_LORA), lambda i: (i, 0)),
        pl.BlockSpec((tm, LANES), lambda i: (i, 0)),
        pl.BlockSpec((KV_LORA, MLA_HEADS * (LANES + MLA_V)), lambda i: (0, 0)),
        pl.BlockSpec((1, LANES), lambda i: (0, 0)),
    ]
    args = [ckvn, kr128, wts["w_ukv"], wts["mla_k_g"]]
    if rope:
        in_specs += [pl.BlockSpec((tm, LANES), lambda i: (i % per_seq, 0))] * 2
        args += [tabs[2], tabs[3]]
    return pl.pallas_call(
        functools.partial(_mlakv_kernel, rope=rope),
        grid=(t // tm,),
        in_specs=in_specs,
        out_specs=[pl.BlockSpec((tm, MLA_HEADS * LANES), lambda i: (i, 0)),
                   pl.BlockSpec((tm, MLA_HEADS * MLA_V), lambda i: (i, 0))],
        out_shape=[jax.ShapeDtypeStruct((t, MLA_HEADS * LANES), _BF),
                   jax.ShapeDtypeStruct((t, MLA_HEADS * MLA_V), _BF)],
        compiler_params=_cparams("arbitrary"),
        name="mlakv_rope" if rope else "mlakv",
    )(*args)


_CHUNK = 8


def _lru_kernel(xb_ref, yb_ref, h0_ref, cw_ref, cb_ref, wg_ref, bg_ref, lam_ref,
                ob_ref, st_ref, af_ref, bf_ref, ab_ref, bb_ref, h_ref, *, n):
    x = xb_ref[...]
    rows = lax.broadcasted_iota(jnp.int32, x.shape, 0)
    xc = cb_ref[...] + x * cw_ref[2:3, :]
    xc = xc + jnp.where(rows >= 2, pltpu.roll(x, 2, 0), 0.0) * cw_ref[0:1, :]
    xc = xc + jnp.where(rows >= 1, pltpu.roll(x, 1, 0), 0.0) * cw_ref[1:2, :]
    xc = xc + jnp.where(rows < n - 1, pltpu.roll(x, n - 1, 0), 0.0) * cw_ref[3:4, :]

    xcb = xc.astype(_BF)
    half = LRU_WIDTH // 2
    pre = [jnp.dot(xcb[:, j * half:(j + 1) * half], wg_ref[j], preferred_element_type=_F32) for j in range(2)]
    for d, (a_ref, b_ref) in enumerate(((af_ref, bf_ref), (ab_ref, bb_ref))):
        pa = jnp.concatenate([pre[0][:, (2 * d) * half:(2 * d + 1) * half],
                              pre[1][:, (2 * d) * half:(2 * d + 1) * half]], axis=-1)
        px = jnp.concatenate([pre[0][:, (2 * d + 1) * half:(2 * d + 2) * half],
                              pre[1][:, (2 * d + 1) * half:(2 * d + 2) * half]], axis=-1)
        r = jax.nn.sigmoid(pa + bg_ref[2 * d:2 * d + 1, :])
        i = jax.nn.sigmoid(px + bg_ref[2 * d + 1:2 * d + 2, :])
        nlam = -lam_ref[d:d + 1, :]
        softplus = jnp.maximum(nlam, 0.0) + jnp.log1p(jnp.exp(-jnp.abs(nlam)))
        log_a = (-LRU_C) * r * softplus
        a = jnp.exp(log_a)
        a_ref[...] = a
        b_ref[...] = jnp.sqrt(-jnp.tanh(log_a) * (a * a + 1.0)) * i * xc

    nchunks = n // _CHUNK
    crow = lax.broadcasted_iota(jnp.int32, (_CHUNK, LRU_WIDTH), 0)

    def chunk_scan(a, b, carry, reverse):
        for d in (1, 2, 4):
            if reverse:
                keep = crow < _CHUNK - d
                a_s = pltpu.roll(a, _CHUNK - d, 0)
                b_s = pltpu.roll(b, _CHUNK - d, 0)
            else:
                keep = crow >= d
                a_s = pltpu.roll(a, d, 0)
                b_s = pltpu.roll(b, d, 0)
            b = jnp.where(keep, a * b_s + b, b)
            a = jnp.where(keep, a * a_s, a)
        return a * carry + b

    def body(ci, carry):
        cf, cb = carry
        off_f = pl.multiple_of(ci * _CHUNK, _CHUNK)
        off_b = pl.multiple_of((nchunks - 1 - ci) * _CHUNK, _CHUNK)
        hf = chunk_scan(af_ref[pl.ds(off_f, _CHUNK), :], bf_ref[pl.ds(off_f, _CHUNK), :], cf, False)
        hb = chunk_scan(ab_ref[pl.ds(off_b, _CHUNK), :], bb_ref[pl.ds(off_b, _CHUNK), :], cb, True)
        bf_ref[pl.ds(off_f, _CHUNK), :] = hf
        bb_ref[pl.ds(off_b, _CHUNK), :] = hb
        return hf[_CHUNK - 1:_CHUNK, :], hb[0:1, :]

    cf, cb = lax.fori_loop(0, nchunks, body, (h0_ref[0:1, :], h0_ref[1:2, :]))
    st_ref[0:1, :] = cf
    st_ref[1:2, :] = cb
    ob_ref[...] = ((bf_ref[...] + bb_ref[...]) * jax.nn.gelu(yb_ref[...])).astype(_BF)


def _lru(xb, yb, h0, wts, *, n):
    nb = xb.shape[0] // n
    seq = lambda b: (b, 0)
    const2 = lambda b: (0, 0)
    return pl.pallas_call(
        functools.partial(_lru_kernel, n=n),
        grid=(nb,),
        in_specs=[
            pl.BlockSpec((n, LRU_WIDTH), seq),
            pl.BlockSpec((n, LRU_WIDTH), seq),
            pl.BlockSpec((None, 2, LRU_WIDTH), lambda b: (b, 0, 0)),
            pl.BlockSpec((CONV_W, LRU_WIDTH), const2),
            pl.BlockSpec((1, LRU_WIDTH), const2),
            pl.BlockSpec((2, LRU_WIDTH // 2, 2 * LRU_WIDTH), lambda b: (0, 0, 0)),
            pl.BlockSpec((4, LRU_WIDTH), const2),
            pl.BlockSpec((2, LRU_WIDTH), const2),
        ],
        out_specs=[pl.BlockSpec((n, LRU_WIDTH), seq),
                   pl.BlockSpec((None, 2, LRU_WIDTH), lambda b: (b, 0, 0))],
        out_shape=[jax.ShapeDtypeStruct((nb * n, LRU_WIDTH), _BF),
                   jax.ShapeDtypeStruct((nb, 2, LRU_WIDTH), _F32)],
        scratch_shapes=[pltpu.VMEM((n, LRU_WIDTH), _F32) for _ in range(5)],
        compiler_params=_cparams("arbitrary"),
        name=f"lru_{n}",
    )(xb, yb, h0, wts["conv_w"], wts["conv_b"], wts["lru_wg"], wts["lru_bg"], wts["lru_lam"])


def _attn_kernel(*refs, heads, kv_heads, dv, cached):
    if cached:
        q_ref, k_ref, v_ref, kc_ref, vc_ref, o_ref = refs
    else:
        q_ref, k_ref, v_ref, o_ref = refs
    group = heads // kv_heads
    contract_last = (((1,), (1,)), ((), ()))
    outs = []
    for hd in range(heads):
        g = hd // group
        q = q_ref[:, hd * LANES:(hd + 1) * LANES]
        s = lax.dot_general(q, k_ref[:, g * LANES:(g + 1) * LANES], contract_last, preferred_element_type=_F32)
        m = jnp.max(s, axis=-1, keepdims=True)
        if cached:
            sc = lax.dot_general(q, kc_ref[:, g * LANES:(g + 1) * LANES], contract_last,
                                 preferred_element_type=_F32)
            m = jnp.maximum(m, jnp.max(sc, axis=-1, keepdims=True))
        p = jnp.exp(s - m)
        den = jnp.sum(p, axis=-1, keepdims=True)
        o = jnp.dot(p.astype(_BF), v_ref[:, g * dv:(g + 1) * dv], preferred_element_type=_F32)
        if cached:
            pc = jnp.exp(sc - m)
            den = den + jnp.sum(pc, axis=-1, keepdims=True)
            o = o + jnp.dot(pc.astype(_BF), vc_ref[:, g * dv:(g + 1) * dv], preferred_element_type=_F32)
        outs.append(o / den)
    o_ref[...] = jnp.concatenate(outs, axis=-1).astype(_BF)


def _attn(q, k, v, kc, vc, *, heads, kv_heads, dv, n, tq, name):
    t = q.shape[0]
    nb = t // n
    nq = n // tq
    cached = kc is not None
    in_specs = [
        pl.BlockSpec((tq, heads * LANES), lambda b, i: (b * nq + i, 0)),
        pl.BlockSpec((n, kv_heads * LANES), lambda b, i: (b, 0)),
        pl.BlockSpec((n, kv_heads * dv), lambda b, i: (b, 0)),
    ]
    args = [q, k, v]
    if cached:
        nc = kc.shape[0] // nb
        in_specs += [pl.BlockSpec((nc, kv_heads * LANES), lambda b, i: (b, 0)),
                     pl.BlockSpec((nc, kv_heads * dv), lambda b, i: (b, 0))]
        args += [kc, vc]
    return pl.pallas_call(
        functools.partial(_attn_kernel, heads=heads, kv_heads=kv_heads, dv=dv, cached=cached),
        grid=(nb, nq),
        in_specs=in_specs,
        out_specs=pl.BlockSpec((tq, heads * dv), lambda b, i: (b * nq + i, 0)),
        out_shape=jax.ShapeDtypeStruct((t, heads * dv), _BF),
        compiler_params=_cparams("arbitrary", "arbitrary"),
        name=name,
    )(*args)


def _post_kernel(oac_ref, obc_ref, occ_ref, xc_ref, oal_ref, obl_ref, ocl_ref, xl_ref,
                 mod_ref, wout_ref, n2g_ref, rw_ref, rb_ref,
                 x1c_ref, x1l_ref, h2_ref, tidx_ref, tw_ref, hist_ref, *, tm, ctx_tiles):
    i = pl.program_id(0)

    @pl.when(i < ctx_tiles)
    def _():
        _post_block(oac_ref, obc_ref, occ_ref, xc_ref, mod_ref, wout_ref, n2g_ref, rw_ref, rb_ref,
                    x1c_ref, h2_ref, tidx_ref, tw_ref, hist_ref, tm)

    @pl.when(i >= ctx_tiles)
    def _():
        _post_block(oal_ref, obl_ref, ocl_ref, xl_ref, mod_ref, wout_ref, n2g_ref, rw_ref, rb_ref,
                    x1l_ref, h2_ref, tidx_ref, tw_ref, hist_ref, tm)


def _post_block(oa_ref, ob_ref, oc_ref, x_ref, mod_ref, wout_ref, n2g_ref, rw_ref, rb_ref,
                x1_ref, h2_ref, tidx_ref, tw_ref, hist_ref, tm):
    w0 = A_HEADS * A_HEAD_DIM
    w1 = w0 + LRU_WIDTH
    o = jnp.dot(oa_ref[...], wout_ref[0:w0, :], preferred_element_type=_F32)
    o = o + jnp.dot(ob_ref[...], wout_ref[w0:w1, :], preferred_element_type=_F32)
    o = o + jnp.dot(oc_ref[...], wout_ref[w1:, :], preferred_element_type=_F32)
    x1 = x_ref[...] + mod_ref[2:3, :] * o
    x1_ref[...] = x1
    xn = x1 * lax.rsqrt(jnp.mean(x1 * x1, axis=-1, keepdims=True) + EPS) * n2g_ref[...]
    h2 = xn * (1.0 + mod_ref[4:5, :]) + mod_ref[3:4, :]
    for c in range(_ROW_CHUNKS):
        h2_ref[pl.ds(c, tm, stride=_ROW_CHUNKS), :] = h2[:, c * LANES:(c + 1) * LANES]

    logits = _split_dot(h2, rw_ref[...]) + rb_ref[...]
    lane = lax.broadcasted_iota(jnp.int32, logits.shape, 1).astype(_F32)
    neg = jnp.float32(-jnp.inf)
    work = logits
    top = None
    tidx = jnp.zeros(logits.shape, _F32)
    tw = jnp.zeros(logits.shape, _F32)
    for k in range(TOP_K):
        m = jnp.max(work, axis=-1, keepdims=True)
        if top is None:
            top = m
        first = jnp.min(jnp.where(work == m, lane, float(LANES)), axis=-1, keepdims=True)
        work = jnp.where(lane == first, neg, work)
        tidx = jnp.where(lane == float(k), first, tidx)
        tw = jnp.where(lane == float(k), jnp.exp(m - top), tw)
    tidx_ref[...] = tidx.astype(jnp.int32)
    tw_ref[...] = tw / jnp.sum(tw, axis=-1, keepdims=True)
    hist_ref[...] = jnp.sum(jnp.where(work == neg, 1.0, 0.0), axis=0, keepdims=True)


_ROW_CHUNKS = D_MODEL // LANES


def _post(ctx_ops, lat_ops, mods_l, wts, *, tm, seq):
    ctx_tiles = ctx_ops[3].shape[0] // tm
    lat_tiles = lat_ops[3].shape[0] // tm
    nt = ctx_tiles + lat_tiles
    t = nt * tm
    per_seq = seq // tm
    row = lambda i: (i, 0)
    ctx = lambda i: (jnp.minimum(i, ctx_tiles - 1), 0)
    lat = lambda i: (jnp.maximum(i - ctx_tiles, 0), 0)
    const2 = lambda i: (0, 0)
    mod_idx = lambda i: (jnp.where(i < ctx_tiles, 0, 1 + (i - ctx_tiles) // per_seq), 0, 0)
    mix = A_HEADS * A_HEAD_DIM + LRU_WIDTH + MLA_HEADS * MLA_V
    widths = (A_HEADS * A_HEAD_DIM, LRU_WIDTH, MLA_HEADS * MLA_V, D_MODEL)
    in_specs = ([pl.BlockSpec((tm, w), ctx) for w in widths] + [pl.BlockSpec((tm, w), lat) for w in widths] + [
        pl.BlockSpec((None, 6, D_MODEL), mod_idx),
        pl.BlockSpec((mix, D_MODEL), const2),
        pl.BlockSpec((1, D_MODEL), const2),
        pl.BlockSpec((D_MODEL, LANES), const2),
        pl.BlockSpec((1, LANES), const2),
    ])
    return pl.pallas_call(
        functools.partial(_post_kernel, tm=tm, ctx_tiles=ctx_tiles),
        grid=(nt,),
        in_specs=in_specs,
        out_specs=[pl.BlockSpec((tm, D_MODEL), ctx), pl.BlockSpec((tm, D_MODEL), lat),
                   pl.BlockSpec((tm * _ROW_CHUNKS, LANES), row),
                   pl.BlockSpec((tm, LANES), row), pl.BlockSpec((tm, LANES), row),
                   pl.BlockSpec((None, 1, LANES), lambda i: (i, 0, 0))],
        out_shape=[jax.ShapeDtypeStruct((ctx_tiles * tm, D_MODEL), _F32),
                   jax.ShapeDtypeStruct((lat_tiles * tm, D_MODEL), _F32),
                   jax.ShapeDtypeStruct((t * _ROW_CHUNKS, LANES), _F32),
                   jax.ShapeDtypeStruct((t, LANES), jnp.int32), jax.ShapeDtypeStruct((t, LANES), _F32),
                   jax.ShapeDtypeStruct((nt, 1, LANES), _F32)],
        compiler_params=_cparams("arbitrary"),
        name="post",
    )(*ctx_ops, *lat_ops, mods_l, wts["w_out"], wts["norm2_g"], wts["router_w"], wts["router_b"])


MOE_TM = 256
_K_SHIFT = TOP_K.bit_length() - 1
assert 1 << _K_SHIFT == TOP_K


def _route(flat_e, cnt):
    a = flat_e.shape[0]
    assert a & (a - 1) == 0
    nts = a // MOE_TM + N_EXPERTS
    key = jnp.sort(flat_e * a + jnp.arange(a, dtype=jnp.int32))
    order = key & (a - 1)
    tiles = (cnt + MOE_TM - 1) // MOE_TM
    tile_end = jnp.cumsum(tiles)
    tile_start = tile_end - tiles
    grp_start = jnp.cumsum(cnt) - cnt
    n_tiles = tile_end[-1]
    tile_id = jnp.arange(nts, dtype=jnp.int32)
    tile_e = jnp.minimum(jnp.sum((tile_id[:, None] >= tile_end[None, :]).astype(jnp.int32), axis=1), N_EXPERTS - 1)
    tile_e = jnp.where(tile_id < n_tiles, tile_e, tile_e[jnp.maximum(n_tiles - 1, 0)])
    rank = (tile_id - tile_start[tile_e])[:, None] * MOE_TM + jnp.arange(MOE_TM, dtype=jnp.int32)[None, :]
    valid = (rank < cnt[tile_e][:, None]) & (tile_id < n_tiles)[:, None]
    j = jnp.clip(grp_start[tile_e][:, None] + rank, 0, a - 1)
    code = jnp.where(valid, order[j], 0).reshape(nts * MOE_TM)
    tile_rows = jnp.sum(valid.astype(jnp.int32), axis=1)
    return (tile_e.astype(jnp.int32), tile_rows.astype(jnp.int32), n_tiles.reshape(1).astype(jnp.int32),
            code.astype(jnp.int32))


def _ffn_kernel(te_ref, rows_ref, nt_ref, code_ref, h2_hbm, w1_ref, b1_ref, w2_ref, b2_ref, yk_hbm,
                xbuf, ybuf, w1b, w2b, gsem, ssem, *, t):
    i = pl.program_id(0)
    nts = pl.num_programs(0)
    nt = nt_ref[0]
    slot = i % 2

    rc = _ROW_CHUNKS

    def tile_rows(ref, row):
        return ref.at[pl.ds(pl.multiple_of(row * rc, rc), rc)]

    def gather_copy(tok, r, s):
        return pltpu.make_async_copy(tile_rows(h2_hbm, tok), tile_rows(xbuf.at[s], r), gsem.at[s])

    def scatter_copy(dst, r, s):
        return pltpu.make_async_copy(tile_rows(ybuf.at[s], r), tile_rows(yk_hbm, dst), ssem.at[s])

    def issue_gather(tile, s):
        base = tile * MOE_TM

        def body(r, carry):
            gather_copy(lax.shift_right_logical(code_ref[base + r], _K_SHIFT), r, s).start()
            return carry

        lax.fori_loop(0, MOE_TM, body, 0, unroll=8)

    def wait_gather(s):
        pltpu.make_async_copy(h2_hbm.at[pl.ds(0, MOE_TM * rc)], xbuf.at[s], gsem.at[s]).wait()

    def wait_scatter(s, n_rows):
        size = MOE_TM
        while size >= 1:
            @pl.when((n_rows & size) != 0)
            def _(size=size):
                pltpu.make_async_copy(ybuf.at[s, pl.ds(0, size * rc)], yk_hbm.at[pl.ds(0, size * rc)],
                                      ssem.at[s]).wait()

            size //= 2

    @pl.when(i == 0)
    def _():
        issue_gather(0, 0)

    @pl.when(i + 1 < nt)
    def _():
        issue_gather(i + 1, 1 - slot)

    @pl.when(i < nt)
    def _():
        e_changed = jnp.logical_or(i == 0, te_ref[i] != te_ref[jnp.maximum(i - 1, 0)])

        @pl.when(e_changed)
        def _():
            w1b[...] = w1_ref[...].astype(_BF)
            w2b[...] = w2_ref[...].astype(_BF)

        wait_gather(slot)
        x = jnp.concatenate([xbuf[slot, pl.ds(c, MOE_TM, stride=rc), :].astype(_BF) for c in range(rc)], axis=-1)
        gu = jnp.dot(x, w1b[...], preferred_element_type=_F32) + b1_ref[...]
        gt = jnp.minimum(gu[:, :D_FF], SWIGLU_LIMIT)
        up = jnp.clip(gu[:, D_FF:], -SWIGLU_LIMIT, SWIGLU_LIMIT)
        act = gt * jax.nn.sigmoid(SWIGLU_ALPHA * gt) * (up + 1.0)
        y = jnp.dot(act.astype(_BF), w2b[...], preferred_element_type=_F32) + b2_ref[...]

        @pl.when(i >= 2)
        def _():
            wait_scatter(slot, rows_ref[jnp.maximum(i - 2, 0)])

        for c in range(rc):
            ybuf[slot, pl.ds(c, MOE_TM, stride=rc), :] = y[:, c * LANES:(c + 1) * LANES]
        base = i * MOE_TM
        n_rows = rows_ref[i]

        def body(r, carry):
            code = code_ref[base + r]
            dst = (code & (TOP_K - 1)) * t + lax.shift_right_logical(code, _K_SHIFT)
            scatter_copy(dst, r, slot).start()
            return carry

        @pl.when(n_rows == MOE_TM)
        def _():
            lax.fori_loop(0, MOE_TM, body, 0, unroll=8)

        @pl.when(n_rows < MOE_TM)
        def _():
            lax.fori_loop(0, n_rows, body, 0)

    @pl.when(i == nts - 1)
    def _():
        @pl.when(nt >= 1)
        def _():
            wait_scatter((nt - 1) % 2, rows_ref[jnp.maximum(nt - 1, 0)])

        @pl.when(nt >= 2)
        def _():
            wait_scatter(nt % 2, rows_ref[jnp.maximum(nt - 2, 0)])


def _ffn(h2, tile_e, tile_rows, n_tiles, code, moe_w1, moe_w2, wts, layer):
    t = h2.shape[0] // _ROW_CHUNKS
    nts = tile_e.shape[0]
    by_expert = lambda i, te, rows, nt, code: (te[i], 0, 0)
    by_layer_expert = lambda i, te, rows, nt, code: (layer, te[i], 0, 0)
    grid_spec = pltpu.PrefetchScalarGridSpec(
        num_scalar_prefetch=4,
        grid=(nts,),
        in_specs=[
            pl.BlockSpec(memory_space=pl.ANY),
            pl.BlockSpec((None, None, D_MODEL, 2 * D_FF), by_layer_expert),
            pl.BlockSpec((None, 1, 2 * D_FF), by_expert),
            pl.BlockSpec((None, None, D_FF, D_MODEL), by_layer_expert),
            pl.BlockSpec((None, 1, D_MODEL), by_expert),
        ],
        out_specs=pl.BlockSpec(memory_space=pl.ANY),
        scratch_shapes=[
            pltpu.VMEM((2, MOE_TM * _ROW_CHUNKS, LANES), _F32),
            pltpu.VMEM((2, MOE_TM * _ROW_CHUNKS, LANES), _F32),
            pltpu.VMEM((D_MODEL, 2 * D_FF), _BF),
            pltpu.VMEM((D_FF, D_MODEL), _BF),
            pltpu.SemaphoreType.DMA((2,)),
            pltpu.SemaphoreType.DMA((2,)),
        ],
    )
    return pl.pallas_call(
        functools.partial(_ffn_kernel, t=t),
        grid_spec=grid_spec,
        out_shape=jax.ShapeDtypeStruct((TOP_K * t * _ROW_CHUNKS, LANES), _F32),
        compiler_params=_cparams("arbitrary"),
        name="moe_ffn",
    )(tile_e, tile_rows, n_tiles, code, h2, moe_w1, wts["moe_b1"], moe_w2, wts["moe_b2"])


def _combine_kernel(x1c_ref, x1l_ref, tw_ref, mod_ref, y0_ref, y1_ref, y2_ref, y3_ref, oc_ref, ol_ref,
                    *, tm, ctx_tiles):
    y_refs = (y0_ref, y1_ref, y2_ref, y3_ref)

    def run(x1_ref, o_ref):
        tw = tw_ref[...]
        wk = [jnp.broadcast_to(tw[:, k:k + 1], (tm, LANES)) for k in range(TOP_K)]
        for c in range(_ROW_CHUNKS):
            cols = slice(c * LANES, (c + 1) * LANES)
            acc = wk[0] * y_refs[0][pl.ds(c, tm, stride=_ROW_CHUNKS), :]
            for k in range(1, TOP_K):
                acc = acc + wk[k] * y_refs[k][pl.ds(c, tm, stride=_ROW_CHUNKS), :]
            o_ref[:, cols] = x1_ref[:, cols] + mod_ref[5:6, cols] * acc

    i = pl.program_id(0)

    @pl.when(i < ctx_tiles)
    def _():
        run(x1c_ref, oc_ref)

    @pl.when(i >= ctx_tiles)
    def _():
        run(x1l_ref, ol_ref)


def _combine(x1c, x1l, tw, mods_l, yk, *, tm, seq):
    ctx_tiles = x1c.shape[0] // tm
    lat_tiles = x1l.shape[0] // tm
    nt = ctx_tiles + lat_tiles
    per_seq = seq // tm
    ctx = lambda i: (jnp.minimum(i, ctx_tiles - 1), 0)
    lat = lambda i: (jnp.maximum(i - ctx_tiles, 0), 0)
    mod_idx = lambda i: (jnp.where(i < ctx_tiles, 0, 1 + (i - ctx_tiles) // per_seq), 0, 0)
    plane = lambda k: (lambda i: (k * nt + i, 0))
    return pl.pallas_call(
        functools.partial(_combine_kernel, tm=tm, ctx_tiles=ctx_tiles),
        grid=(nt,),
        in_specs=[pl.BlockSpec((tm, D_MODEL), ctx), pl.BlockSpec((tm, D_MODEL), lat),
                  pl.BlockSpec((tm, LANES), lambda i: (i, 0)),
                  pl.BlockSpec((None, 6, D_MODEL), mod_idx)]
                 + [pl.BlockSpec((tm * _ROW_CHUNKS, LANES), plane(k)) for k in range(TOP_K)],
        out_specs=[pl.BlockSpec((tm, D_MODEL), ctx), pl.BlockSpec((tm, D_MODEL), lat)],
        out_shape=[jax.ShapeDtypeStruct(x1c.shape, _F32), jax.ShapeDtypeStruct(x1l.shape, _F32)],
        compiler_params=_cparams("arbitrary"),
        name="moe_combine",
    )(x1c, x1l, tw, mods_l, yk, yk, yk, yk)


def _pad_heads(w, heads, dim):
    lead = w.shape[:-1]
    w = w.reshape(lead + (heads, dim))
    w = jnp.pad(w, [(0, 0)] * len(lead) + [(0, 0), (0, LANES - dim)])
    return w.reshape(lead + (heads * LANES,))


def _pad_lanes(g, left=0):
    return jnp.pad(g, (left, LANES - left - g.shape[0])).reshape(1, LANES)


def _block_diag_halves(w):
    w4 = w.reshape(2, 4, 64, 64)
    eye = jnp.eye(4, dtype=w.dtype)
    return jnp.einsum("jaik,ab->jaibk", w4, eye).reshape(2, 256, 256)


def _layer_weights(l, p):
    w_in = p["w_in"][l]
    qa, ka, va, xb, yb, cq, ckv, kr = jnp.split(w_in, (512, 640, 768, 1280, 1792, 2176, 2432), axis=-1)
    kr128 = jnp.pad(kr, ((0, 0), (MLA_NOPE, LANES - MLA_NOPE - MLA_ROPE)))
    w_in_p = jnp.concatenate([_pad_heads(qa, A_HEADS, A_HEAD_DIM), _pad_heads(ka, A_KV_HEADS, A_HEAD_DIM), va, xb,
                              yb, cq, ckv, kr128], axis=-1).astype(_BF)
    ukv = p["mla_w_ukv"][l].reshape(KV_LORA, MLA_HEADS, MLA_NOPE + MLA_V)
    w_ukv = jnp.concatenate([_pad_heads(ukv[:, :, :MLA_NOPE].reshape(KV_LORA, -1), MLA_HEADS, MLA_NOPE),
                             ukv[:, :, MLA_NOPE:].reshape(KV_LORA, -1)], axis=-1).astype(_BF)
    wg = jnp.concatenate([_block_diag_halves(p["lru_wa"][l, 0]), _block_diag_halves(p["lru_wx"][l, 0]),
                          _block_diag_halves(p["lru_wa"][l, 1]), _block_diag_halves(p["lru_wx"][l, 1])],
                         axis=-1).astype(_BF)
    bg = jnp.stack([p["lru_ba"][l, 0], p["lru_bx"][l, 0], p["lru_ba"][l, 1], p["lru_bx"][l, 1]])
    return {
        "norm1_g": p["norm1_g"][l].reshape(1, -1),
        "norm2_g": p["norm2_g"][l].reshape(1, -1),
        "w_in": w_in_p,
        "a_q_g": _pad_lanes(p["a_q_g"][l]),
        "a_k_g": _pad_lanes(p["a_k_g"][l]),
        "mla_cq_g": p["mla_cq_g"][l].reshape(1, -1),
        "w_uq": _pad_heads(p["mla_w_uq"][l], MLA_HEADS, MLA_QK).astype(_BF),
        "mla_q_g": _pad_lanes(p["mla_q_g"][l]),
        "mla_ckv_g": p["mla_ckv_g"][l].reshape(1, -1),
        "w_ukv": w_ukv,
        "mla_k_g": _pad_lanes(p["mla_k_g"][l]),
        "conv_w": p["conv_w"][l],
        "conv_b": p["conv_b"][l].reshape(1, -1),
        "lru_wg": wg,
        "lru_bg": bg,
        "lru_lam": p["lru_lam"][l],
        "w_out": p["w_out"][l].astype(_BF),
        "router_w": jnp.pad(p["router_w"][l], ((0, 0), (0, LANES - N_EXPERTS))),
        "router_b": jnp.pad(p["router_b"][l], (0, LANES - N_EXPERTS), constant_values=-1e30).reshape(1, LANES),
        "moe_b1": p["moe_b1"][l].reshape(N_EXPERTS, 1, 2 * D_FF),
        "moe_b2": p["moe_b2"][l].reshape(N_EXPERTS, 1, D_MODEL),
    }


def _rope_tables(n, rot_dim, lane0):
    rows = n // GRID_W
    r = jnp.repeat(jnp.arange(rows, dtype=_F32), GRID_W)
    col = jnp.tile(jnp.arange(GRID_W, dtype=_F32), rows)
    n_freq = rot_dim // 4
    inv = ROPE_THETA ** (-jnp.arange(n_freq, dtype=_F32) / n_freq)
    ang = jnp.concatenate([r[:, None] * inv, col[:, None] * inv], axis=-1)
    cos = jnp.repeat(jnp.cos(ang), 2, axis=-1)
    sin = jnp.repeat(jnp.sin(ang), 2, axis=-1) * jnp.tile(jnp.array([-1.0, 1.0], _F32), rot_dim // 2)
    pad = ((0, 0), (lane0, LANES - lane0 - rot_dim))
    return jnp.pad(cos, pad, constant_values=1.0), jnp.pad(sin, pad)


def kernel(x_prompt, x_sample, cache_attn_k, cache_attn_v, state_lru, cache_mla_ckv, cache_mla_krope, c, c_ctx,
           ada_w, ada_b, norm1_g, norm2_g, w_in, a_q_g, a_k_g, conv_w, conv_b, lru_wa, lru_ba, lru_wx, lru_bx,
           lru_lam, mla_cq_g, mla_w_uq, mla_q_g, mla_ckv_g, mla_w_ukv, mla_k_g, w_out, router_w, router_b,
           moe_w1, moe_b1, moe_w2, moe_b2):
    p = dict(norm1_g=norm1_g, norm2_g=norm2_g, w_in=w_in, a_q_g=a_q_g, a_k_g=a_k_g, conv_w=conv_w, conv_b=conv_b,
             lru_wa=lru_wa, lru_ba=lru_ba, lru_wx=lru_wx, lru_bx=lru_bx, lru_lam=lru_lam, mla_cq_g=mla_cq_g,
             mla_w_uq=mla_w_uq, mla_q_g=mla_q_g, mla_ckv_g=mla_ckv_g, mla_w_ukv=mla_w_ukv, mla_k_g=mla_k_g,
             w_out=w_out, router_w=router_w, router_b=router_b, moe_w1=moe_w1, moe_b1=moe_b1, moe_w2=moe_w2,
             moe_b2=moe_b2)
    batch, seq, _ = x_prompt.shape
    dec_batch, dec_seq, _ = x_sample.shape
    past = cache_attn_k.shape[2]

    cvec = jnp.concatenate([c_ctx[None, :], c, jnp.zeros((N_MODS - 1 - dec_batch, D_MODEL), _F32)], axis=0)
    mods = _ada_mods(cvec, ada_w, ada_b)
    cos_a, sin_a = _rope_tables(dec_seq, A_HEAD_DIM, 0)
    cos_c, sin_c = _rope_tables(dec_seq, MLA_ROPE, MLA_NOPE)
    tabs = (cos_a, sin_a, cos_c, sin_c)

    yp = x_prompt.reshape(batch * seq, D_MODEL)
    ys = x_sample.reshape(dec_batch * dec_seq, D_MODEL)
    zero_state = jnp.zeros((batch, 2, LRU_WIDTH), _F32)
    ks_l, vs_l, hs_l, ckv_l, kr_l = [], [], [], [], []
    tm = 256
    n_tok = batch * seq + dec_batch * dec_seq
    for l in range(DEPTH):
        wts = _layer_weights(l, p)
        mods_l = mods[l]

        qa, kab, kaf, vab, vaf, xb, yb, qc, ckvn, kr128, kr = _proj(yp, mods_l, wts, None, rope=False, tm=tm, seq=seq)
        kc, vc = _mlakv(ckvn, kr128, wts, None, rope=False, tm=tm, seq=seq)
        ob, st = _lru(xb, yb, zero_state, wts, n=seq)
        oa = _attn(qa, kab, vab, None, None, heads=A_HEADS, kv_heads=A_KV_HEADS, dv=A_HEAD_DIM, n=seq, tq=seq,
                   name="attn_a_ctx")
        oc = _attn(qc, kc, vc, None, None, heads=MLA_HEADS, kv_heads=MLA_HEADS, dv=MLA_V, n=seq, tq=seq,
                   name="attn_c_ctx")
        ctx_ops = (oa, ob, oc, yp)
        ks_l.append(kaf.reshape(batch, seq, A_KV_HEADS, A_HEAD_DIM))
        vs_l.append(vaf.reshape(batch, seq, A_KV_HEADS, A_HEAD_DIM))
        hs_l.append(st)
        ckv_l.append(ckvn.reshape(batch, seq, KV_LORA))
        kr_l.append(kr.reshape(batch, seq, MLA_ROPE))

        qa, kab, _, vab, _, xb, yb, qc, ckvn, kr128, _ = _proj(ys, mods_l, wts, tabs, rope=True, tm=tm, seq=dec_seq)
        kc, vc = _mlakv(ckvn, kr128, wts, tabs, rope=True, tm=tm, seq=dec_seq)
        c_kr128 = jnp.pad(cache_mla_krope[:, l].reshape(dec_batch * past, MLA_ROPE),
                          ((0, 0), (MLA_NOPE, LANES - MLA_NOPE - MLA_ROPE)))
        kx, vx = _mlakv(cache_mla_ckv[:, l].reshape(dec_batch * past, KV_LORA), c_kr128, wts, None, rope=False,
                        tm=tm, seq=past)
        ob, _ = _lru(xb, yb, state_lru[:, l], wts, n=dec_seq)
        c_k = _pad_heads(cache_attn_k[:, l].reshape(dec_batch * past, A_KV_HEADS * A_HEAD_DIM), A_KV_HEADS,
                         A_HEAD_DIM).astype(_BF)
        c_v = cache_attn_v[:, l].reshape(dec_batch * past, A_KV_HEADS * A_HEAD_DIM).astype(_BF)
        oa = _attn(qa, kab, vab, c_k, c_v, heads=A_HEADS, kv_heads=A_KV_HEADS, dv=A_HEAD_DIM, n=dec_seq, tq=256,
                   name="attn_a_lat")
        oc = _attn(qc, kc, vc, kx, vx, heads=MLA_HEADS, kv_heads=MLA_HEADS, dv=MLA_V, n=dec_seq, tq=256,
                   name="attn_c_lat")
        x1c, x1l, h2, tidx, tw, hist = _post(ctx_ops, (oa, ob, oc, ys), mods_l, wts, tm=tm, seq=dec_seq)
        flat_e = tidx[:, :TOP_K].reshape(n_tok * TOP_K)
        cnt = jnp.sum(hist, axis=(0, 1))[:N_EXPERTS].astype(jnp.int32)
        tile_e, tile_rows, n_tiles, code = _route(flat_e, cnt)
        yk = _ffn(h2, tile_e, tile_rows, n_tiles, code, moe_w1, moe_w2, wts, l)
        yp, ys = _combine(x1c, x1l, tw, mods_l, yk, tm=tm, seq=dec_seq)

    return (yp.reshape(batch, seq, D_MODEL), ys.reshape(dec_batch, dec_seq, D_MODEL),
            jnp.stack(ks_l, axis=1), jnp.stack(vs_l, axis=1), jnp.stack(hs_l, axis=1),
            jnp.stack(ckv_l, axis=1), jnp.stack(kr_l, axis=1))
```

```python
import functools

import jax
import jax.numpy as jnp
from jax import lax
from jax.experimental import pallas as pl
from jax.experimental.pallas import tpu as pltpu

D_MODEL = 1024
DEPTH = 2
GRID_W = 64
ROPE_THETA = 10000.0
EPS = 1e-6
A_HEADS = 8
A_KV_HEADS = 2
A_HEAD_DIM = 64
LRU_WIDTH = 512
LRU_BLOCKS = 8
CONV_W = 4
LRU_C = 8.0
MLA_HEADS = 8
MLA_NOPE = 64
MLA_ROPE = 32
MLA_QK = MLA_NOPE + MLA_ROPE
MLA_V = 64
Q_LORA = 384
KV_LORA = 256
N_EXPERTS = 32
TOP_K = 4
D_FF = 1024
SWIGLU_LIMIT = 7.0
SWIGLU_ALPHA = 1.702

LANES = 128
N_MODS = 16
VMEM_LIMIT = 56 * 1024 * 1024

_QA0 = 0
_KA0 = _QA0 + A_HEADS * LANES
_VA0 = _KA0 + A_KV_HEADS * LANES
_XB0 = _VA0 + A_KV_HEADS * A_HEAD_DIM
_YB0 = _XB0 + LRU_WIDTH
_CQ0 = _YB0 + LRU_WIDTH
_CKV0 = _CQ0 + Q_LORA
_KR0 = _CKV0 + KV_LORA
_NP = _KR0 + LANES

_BF = jnp.bfloat16
_F32 = jnp.float32


def _cparams(*sem):
    return pltpu.CompilerParams(dimension_semantics=sem, vmem_limit_bytes=VMEM_LIMIT)


def _split_dot(a, b):
    a_hi = a.astype(_BF)
    a_lo = (a - a_hi.astype(_F32)).astype(_BF)
    b_hi = b.astype(_BF)
    b_lo = (b - b_hi.astype(_F32)).astype(_BF)
    d = functools.partial(jnp.dot, preferred_element_type=_F32)
    return d(a_hi, b_hi) + (d(a_hi, b_lo) + d(a_lo, b_hi))


def _rope_rotate(x, cos, sin_signed):
    lane = lax.broadcasted_iota(jnp.int32, x.shape, 1)
    swapped = jnp.where((lane & 1) == 0, pltpu.roll(x, LANES - 1, 1), pltpu.roll(x, 1, 1))
    return x * cos + swapped * sin_signed


def _head_rms(xh, denom):
    return xh * lax.rsqrt(jnp.sum(xh * xh, axis=-1, keepdims=True) * (1.0 / denom) + EPS)


def _mods_kernel(c_ref, w_ref, b_ref, o_ref):
    c = c_ref[...]
    s = c * jax.nn.sigmoid(c)
    o_ref[...] = _split_dot(s, w_ref[...]) + b_ref[...]


def _ada_mods(cvec, ada_w, ada_b):
    tn = 1536
    n_out = ada_w.shape[-1]
    out = pl.pallas_call(
        _mods_kernel,
        grid=(DEPTH, n_out // tn),
        in_specs=[
            pl.BlockSpec((N_MODS, D_MODEL), lambda l, j: (0, 0)),
            pl.BlockSpec((None, D_MODEL, tn), lambda l, j: (l, 0, j)),
            pl.BlockSpec((None, 1, tn), lambda l, j: (l, 0, j)),
        ],
        out_specs=pl.BlockSpec((None, N_MODS, tn), lambda l, j: (l, 0, j)),
        out_shape=jax.ShapeDtypeStruct((DEPTH, N_MODS, n_out), _F32),
        compiler_params=_cparams("arbitrary", "arbitrary"),
        name="ada_mods",
    )(cvec, ada_w, ada_b.reshape(DEPTH, 1, n_out))
    return out.reshape(DEPTH, N_MODS, 6, D_MODEL)


def _proj_kernel(*refs, rope):
    if rope:
        (x_ref, mod_ref, n1g_ref, win_ref, qg_ref, kg_ref, cqg_ref, wuq_ref, mqg_ref, ckvg_ref,
         cosa_ref, sina_ref, cosc_ref, sinc_ref,
         qa_ref, kab_ref, kaf_ref, vab_ref, vaf_ref, xb_ref, yb_ref, qc_ref, ckvn_ref, kr128_ref, kr_ref) = refs
    else:
        (x_ref, mod_ref, n1g_ref, win_ref, qg_ref, kg_ref, cqg_ref, wuq_ref, mqg_ref, ckvg_ref,
         qa_ref, kab_ref, kaf_ref, vab_ref, vaf_ref, xb_ref, yb_ref, qc_ref, ckvn_ref, kr128_ref, kr_ref) = refs
    x = x_ref[...]
    xn = x * lax.rsqrt(jnp.mean(x * x, axis=-1, keepdims=True) + EPS) * n1g_ref[...]
    h = xn * (1.0 + mod_ref[1:2, :]) + mod_ref[0:1, :]
    z = jnp.dot(h.astype(_BF), win_ref[...], preferred_element_type=_F32)

    for hd in range(A_HEADS):
        q = _head_rms(z[:, _QA0 + hd * LANES:_QA0 + (hd + 1) * LANES], A_HEAD_DIM) * qg_ref[...]
        if rope:
            q = _rope_rotate(q, cosa_ref[...], sina_ref[...])
        qa_ref[:, hd * LANES:(hd + 1) * LANES] = (q * (A_HEAD_DIM ** -0.5)).astype(_BF)
    k_heads = []
    for hd in range(A_KV_HEADS):
        k = _head_rms(z[:, _KA0 + hd * LANES:_KA0 + (hd + 1) * LANES], A_HEAD_DIM) * kg_ref[...]
        k_heads.append(k[:, :A_HEAD_DIM])
        if rope:
            k = _rope_rotate(k, cosa_ref[...], sina_ref[...])
        kab_ref[:, hd * LANES:(hd + 1) * LANES] = k.astype(_BF)
    kaf_ref[...] = jnp.concatenate(k_heads, axis=-1)
    va = z[:, _VA0:_VA0 + A_KV_HEADS * A_HEAD_DIM]
    vaf_ref[...] = va
    vab_ref[...] = va.astype(_BF)

    xb_ref[...] = z[:, _XB0:_XB0 + LRU_WIDTH]
    yb_ref[...] = z[:, _YB0:_YB0 + LRU_WIDTH]

    cq = z[:, _CQ0:_CQ0 + Q_LORA]
    cqn = cq * lax.rsqrt(jnp.mean(cq * cq, axis=-1, keepdims=True) + EPS) * cqg_ref[...]
    zq = jnp.dot(cqn.astype(_BF), wuq_ref[...], preferred_element_type=_F32)
    for hd in range(MLA_HEADS):
        q = _head_rms(zq[:, hd * LANES:(hd + 1) * LANES], MLA_QK) * mqg_ref[...]
        if rope:
            q = _rope_rotate(q, cosc_ref[...], sinc_ref[...])
        qc_ref[:, hd * LANES:(hd + 1) * LANES] = (q * (MLA_QK ** -0.5)).astype(_BF)
    ckv = z[:, _CKV0:_CKV0 + KV_LORA]
    ckvn_ref[...] = ckv * lax.rsqrt(jnp.mean(ckv * ckv, axis=-1, keepdims=True) + EPS) * ckvg_ref[...]
    kr128 = z[:, _KR0:_KR0 + LANES]
    kr128_ref[...] = kr128
    kr_ref[...] = kr128[:, MLA_NOPE:MLA_NOPE + MLA_ROPE]


def _proj(x, mods_l, wts, tabs, *, rope, tm, seq):
    t = x.shape[0]
    nt = t // tm
    per_seq = seq // tm

    def row(i):
        return (i, 0)

    def const2(i):
        return (0, 0)

    if rope:
        def mod_idx(i):
            return (1 + i // per_seq, 0, 0)

        def tab_idx(i):
            return (i % per_seq, 0)
    else:
        def mod_idx(i):
            return (0, 0, 0)

    in_specs = [
        pl.BlockSpec((tm, D_MODEL), row),
        pl.BlockSpec((None, 6, D_MODEL), mod_idx),
        pl.BlockSpec((1, D_MODEL), const2),
        pl.BlockSpec((D_MODEL, _NP), const2),
        pl.BlockSpec((1, LANES), const2),
        pl.BlockSpec((1, LANES), const2),
        pl.BlockSpec((1, Q_LORA), const2),
        pl.BlockSpec((Q_LORA, MLA_HEADS * LANES), const2),
        pl.BlockSpec((1, LANES), const2),
        pl.BlockSpec((1, KV_LORA), const2),
    ]
    args = [x, mods_l, wts["norm1_g"], wts["w_in"], wts["a_q_g"], wts["a_k_g"], wts["mla_cq_g"], wts["w_uq"],
            wts["mla_q_g"], wts["mla_ckv_g"]]
    if rope:
        in_specs += [pl.BlockSpec((tm, LANES), tab_idx)] * 4
        args += list(tabs)
    widths = [(A_HEADS * LANES, _BF), (A_KV_HEADS * LANES, _BF), (A_KV_HEADS * A_HEAD_DIM, _F32),
              (A_KV_HEADS * A_HEAD_DIM, _BF), (A_KV_HEADS * A_HEAD_DIM, _F32), (LRU_WIDTH, _F32), (LRU_WIDTH, _F32),
              (MLA_HEADS * LANES, _BF), (KV_LORA, _F32), (LANES, _F32), (MLA_ROPE, _F32)]
    out_specs = [pl.BlockSpec((tm, w), row) for w, _ in widths]
    out_shape = [jax.ShapeDtypeStruct((t, w), dt) for w, dt in widths]
    return pl.pallas_call(
        functools.partial(_proj_kernel, rope=rope),
        grid=(nt,),
        in_specs=in_specs,
        out_specs=out_specs,
        out_shape=out_shape,
        compiler_params=_cparams("arbitrary"),
        name="proj_rope" if rope else "proj",
    )(*args)


def _mlakv_kernel(*refs, rope):
    if rope:
        ckvn_ref, kr128_ref, wukv_ref, kg_ref, cos_ref, sin_ref, k_ref, v_ref = refs
    else:
        ckvn_ref, kr128_ref, wukv_ref, kg_ref, k_ref, v_ref = refs
    kv = jnp.dot(ckvn_ref[...].astype(_BF), wukv_ref[...], preferred_element_type=_F32)
    kr128 = kr128_ref[...]
    for hd in range(MLA_HEADS):
        k = _head_rms(kv[:, hd * LANES:(hd + 1) * LANES] + kr128, MLA_QK) * kg_ref[...]
        if rope:
            k = _rope_rotate(k, cos_ref[...], sin_ref[...])
        k_ref[:, hd * LANES:(hd + 1) * LANES] = k.astype(_BF)
    v_ref[...] = kv[:, MLA_HEADS * LANES:].astype(_BF)


def _mlakv(ckvn, kr128, wts, tabs, *, rope, tm, seq):
    t = ckvn.shape[0]
    per_seq = seq // tm
    in_specs = [
        pl.BlockSpec((tm, KV_LORA), lambda i: (i, 0)),
        pl.BlockSpec((tm, LANES), lambda i: (i, 0)),
        pl.BlockSpec((KV_LORA, MLA_HEADS * (LANES + MLA_V)), lambda i: (0, 0)),
        pl.BlockSpec((1, LANES), lambda i: (0, 0)),
    ]
    args = [ckvn, kr128, wts["w_ukv"], wts["mla_k_g"]]
    if rope:
        in_specs += [pl.BlockSpec((tm, LANES), lambda i: (i % per_seq, 0))] * 2
        args += [tabs[2], tabs[3]]
    return pl.pallas_call(
        functools.partial(_mlakv_kernel, rope=rope),
        grid=(t // tm,),
        in_specs=in_specs,
        out_specs=[pl.BlockSpec((tm, MLA_HEADS * LANES), lambda i: (i, 0)),
                   pl.BlockSpec((tm, MLA_HEADS * MLA_V), lambda i: (i, 0))],
        out_shape=[jax.ShapeDtypeStruct((t, MLA_HEADS * LANES), _BF),
                   jax.ShapeDtypeStruct((t, MLA_HEADS * MLA_V), _BF)],
        compiler_params=_cparams("arbitrary"),
        name="mlakv_rope" if rope else "mlakv",
    )(*args)


_CHUNK = 8


def _lru_kernel(xb_ref, yb_ref, h0_ref, cw_ref, cb_ref, wg_ref, bg_ref, lam_ref,
                ob_ref, st_ref, af_ref, bf_ref, ab_ref, bb_ref, h_ref, *, n):
    x = xb_ref[...]
    rows = lax.broadcasted_iota(jnp.int32, x.shape, 0)
    xc = cb_ref[...] + x * cw_ref[2:3, :]
    xc = xc + jnp.where(rows >= 2, pltpu.roll(x, 2, 0), 0.0) * cw_ref[0:1, :]
    xc = xc + jnp.where(rows >= 1, pltpu.roll(x, 1, 0), 0.0) * cw_ref[1:2, :]
    xc = xc + jnp.where(rows < n - 1, pltpu.roll(x, n - 1, 0), 0.0) * cw_ref[3:4, :]

    xcb = xc.astype(_BF)
    half = LRU_WIDTH // 2
    pre = [jnp.dot(xcb[:, j * half:(j + 1) * half], wg_ref[j], preferred_element_type=_F32) for j in range(2)]
    for d, (a_ref, b_ref) in enumerate(((af_ref, bf_ref), (ab_ref, bb_ref))):
        pa = jnp.concatenate([pre[0][:, (2 * d) * half:(2 * d + 1) * half],
                              pre[1][:, (2 * d) * half:(2 * d + 1) * half]], axis=-1)
        px = jnp.concatenate([pre[0][:, (2 * d + 1) * half:(2 * d + 2) * half],
                              pre[1][:, (2 * d + 1) * half:(2 * d + 2) * half]], axis=-1)
        r = jax.nn.sigmoid(pa + bg_ref[2 * d:2 * d + 1, :])
        i = jax.nn.sigmoid(px + bg_ref[2 * d + 1:2 * d + 2, :])
        nlam = -lam_ref[d:d + 1, :]
        softplus = jnp.maximum(nlam, 0.0) + jnp.log1p(jnp.exp(-jnp.abs(nlam)))
        log_a = (-LRU_C) * r * softplus
        a = jnp.exp(log_a)
        a_ref[...] = a
        b_ref[...] = jnp.sqrt(-jnp.tanh(log_a) * (a * a + 1.0)) * i * xc

    nchunks = n // _CHUNK
    crow = lax.broadcasted_iota(jnp.int32, (_CHUNK, LRU_WIDTH), 0)

    def chunk_scan(a, b, carry, reverse):
        for d in (1, 2, 4):
            if reverse:
                keep = crow < _CHUNK - d
                a_s = pltpu.roll(a, _CHUNK - d, 0)
                b_s = pltpu.roll(b, _CHUNK - d, 0)
            else:
                keep = crow >= d
                a_s = pltpu.roll(a, d, 0)
                b_s = pltpu.roll(b, d, 0)
            b = jnp.where(keep, a * b_s + b, b)
            a = jnp.where(keep, a * a_s, a)
        return a * carry + b

    def body(ci, carry):
        cf, cb = carry
        off_f = pl.multiple_of(ci * _CHUNK, _CHUNK)
        off_b = pl.multiple_of((nchunks - 1 - ci) * _CHUNK, _CHUNK)
        hf = chunk_scan(af_ref[pl.ds(off_f, _CHUNK), :], bf_ref[pl.ds(off_f, _CHUNK), :], cf, False)
        hb = chunk_scan(ab_ref[pl.ds(off_b, _CHUNK), :], bb_ref[pl.ds(off_b, _CHUNK), :], cb, True)
        bf_ref[pl.ds(off_f, _CHUNK), :] = hf
        bb_ref[pl.ds(off_b, _CHUNK), :] = hb
        return hf[_CHUNK - 1:_CHUNK, :], hb[0:1, :]

    cf, cb = lax.fori_loop(0, nchunks, body, (h0_ref[0:1, :], h0_ref[1:2, :]))
    st_ref[0:1, :] = cf
    st_ref[1:2, :] = cb
    ob_ref[...] = ((bf_ref[...] + bb_ref[...]) * jax.nn.gelu(yb_ref[...])).astype(_BF)


def _lru(xb, yb, h0, wts, *, n):
    nb = xb.shape[0] // n
    seq = lambda b: (b, 0)
    const2 = lambda b: (0, 0)
    return pl.pallas_call(
        functools.partial(_lru_kernel, n=n),
        grid=(nb,),
        in_specs=[
            pl.BlockSpec((n, LRU_WIDTH), seq),
            pl.BlockSpec((n, LRU_WIDTH), seq),
            pl.BlockSpec((None, 2, LRU_WIDTH), lambda b: (b, 0, 0)),
            pl.BlockSpec((CONV_W, LRU_WIDTH), const2),
            pl.BlockSpec((1, LRU_WIDTH), const2),
            pl.BlockSpec((2, LRU_WIDTH // 2, 2 * LRU_WIDTH), lambda b: (0, 0, 0)),
            pl.BlockSpec((4, LRU_WIDTH), const2),
            pl.BlockSpec((2, LRU_WIDTH), const2),
        ],
        out_specs=[pl.BlockSpec((n, LRU_WIDTH), seq),
                   pl.BlockSpec((None, 2, LRU_WIDTH), lambda b: (b, 0, 0))],
        out_shape=[jax.ShapeDtypeStruct((nb * n, LRU_WIDTH), _BF),
                   jax.ShapeDtypeStruct((nb, 2, LRU_WIDTH), _F32)],
        scratch_shapes=[pltpu.VMEM((n, LRU_WIDTH), _F32) for _ in range(5)],
        compiler_params=_cparams("arbitrary"),
        name=f"lru_{n}",
    )(xb, yb, h0, wts["conv_w"], wts["conv_b"], wts["lru_wg"], wts["lru_bg"], wts["lru_lam"])


def _attn_kernel(*refs, heads, kv_heads, dv, cached):
    if cached:
        q_ref, k_ref, v_ref, kc_ref, vc_ref, o_ref = refs
    else:
        q_ref, k_ref, v_ref, o_ref = refs
    group = heads // kv_heads
    contract_last = (((1,), (1,)), ((), ()))
    outs = []
    for hd in range(heads):
        g = hd // group
        q = q_ref[:, hd * LANES:(hd + 1) * LANES]
        s = lax.dot_general(q, k_ref[:, g * LANES:(g + 1) * LANES], contract_last, preferred_element_type=_F32)
        m = jnp.max(s, axis=-1, keepdims=True)
        if cached:
            sc = lax.dot_general(q, kc_ref[:, g * LANES:(g + 1) * LANES], contract_last,
                                 preferred_element_type=_F32)
            m = jnp.maximum(m, jnp.max(sc, axis=-1, keepdims=True))
        p = jnp.exp(s - m)
        den = jnp.sum(p, axis=-1, keepdims=True)
        o = jnp.dot(p.astype(_BF), v_ref[:, g * dv:(g + 1) * dv], preferred_element_type=_F32)
        if cached:
            pc = jnp.exp(sc - m)
            den = den + jnp.sum(pc, axis=-1, keepdims=True)
            o = o + jnp.dot(pc.astype(_BF), vc_ref[:, g * dv:(g + 1) * dv], preferred_element_type=_F32)
        outs.append(o / den)
    o_ref[...] = jnp.concatenate(outs, axis=-1).astype(_BF)


def _attn(q, k, v, kc, vc, *, heads, kv_heads, dv, n, tq, name):
    t = q.shape[0]
    nb = t // n
    nq = n // tq
    cached = kc is not None
    in_specs = [
        pl.BlockSpec((tq, heads * LANES), lambda b, i: (b * nq + i, 0)),
        pl.BlockSpec((n, kv_heads * LANES), lambda b, i: (b, 0)),
        pl.BlockSpec((n, kv_heads * dv), lambda b, i: (b, 0)),
    ]
    args = [q, k, v]
    if cached:
        nc = kc.shape[0] // nb
        in_specs += [pl.BlockSpec((nc, kv_heads * LANES), lambda b, i: (b, 0)),
                     pl.BlockSpec((nc, kv_heads * dv), lambda b, i: (b, 0))]
        args += [kc, vc]
    return pl.pallas_call(
        functools.partial(_attn_kernel, heads=heads, kv_heads=kv_heads, dv=dv, cached=cached),
        grid=(nb, nq),
        in_specs=in_specs,
        out_specs=pl.BlockSpec((tq, heads * dv), lambda b, i: (b * nq + i, 0)),
        out_shape=jax.ShapeDtypeStruct((t, heads * dv), _BF),
        compiler_params=_cparams("arbitrary", "arbitrary"),
        name=name,
    )(*args)


def _post_kernel(oac_ref, obc_ref, occ_ref, xc_ref, oal_ref, obl_ref, ocl_ref, xl_ref,
                 mod_ref, wout_ref, n2g_ref, rw_ref, rb_ref,
                 x1c_ref, x1l_ref, h2_ref, tidx_ref, tw_ref, hist_ref, *, tm, ctx_tiles):
    i = pl.program_id(0)

    @pl.when(i < ctx_tiles)
    def _():
        _post_block(oac_ref, obc_ref, occ_ref, xc_ref, mod_ref, wout_ref, n2g_ref, rw_ref, rb_ref,
                    x1c_ref, h2_ref, tidx_ref, tw_ref, hist_ref, tm)

    @pl.when(i >= ctx_tiles)
    def _():
        _post_block(oal_ref, obl_ref, ocl_ref, xl_ref, mod_ref, wout_ref, n2g_ref, rw_ref, rb_ref,
                    x1l_ref, h2_ref, tidx_ref, tw_ref, hist_ref, tm)


def _post_block(oa_ref, ob_ref, oc_ref, x_ref, mod_ref, wout_ref, n2g_ref, rw_ref, rb_ref,
                x1_ref, h2_ref, tidx_ref, tw_ref, hist_ref, tm):
    w0 = A_HEADS * A_HEAD_DIM
    w1 = w0 + LRU_WIDTH
    o = jnp.dot(oa_ref[...], wout_ref[0:w0, :], preferred_element_type=_F32)
    o = o + jnp.dot(ob_ref[...], wout_ref[w0:w1, :], preferred_element_type=_F32)
    o = o + jnp.dot(oc_ref[...], wout_ref[w1:, :], preferred_element_type=_F32)
    x1 = x_ref[...] + mod_ref[2:3, :] * o
    x1_ref[...] = x1
    xn = x1 * lax.rsqrt(jnp.mean(x1 * x1, axis=-1, keepdims=True) + EPS) * n2g_ref[...]
    h2 = xn * (1.0 + mod_ref[4:5, :]) + mod_ref[3:4, :]
    for c in range(_ROW_CHUNKS):
        h2_ref[pl.ds(c, tm, stride=_ROW_CHUNKS), :] = h2[:, c * LANES:(c + 1) * LANES]

    logits = _split_dot(h2, rw_ref[...]) + rb_ref[...]
    lane = lax.broadcasted_iota(jnp.int32, logits.shape, 1).astype(_F32)
    neg = jnp.float32(-jnp.inf)
    work = logits
    top = None
    tidx = jnp.zeros(logits.shape, _F32)
    tw = jnp.zeros(logits.shape, _F32)
    for k in range(TOP_K):
        m = jnp.max(work, axis=-1, keepdims=True)
        if top is None:
            top = m
        first = jnp.min(jnp.where(work == m, lane, float(LANES)), axis=-1, keepdims=True)
        work = jnp.where(lane == first, neg, work)
        tidx = jnp.where(lane == float(k), first, tidx)
        tw = jnp.where(lane == float(k), jnp.exp(m - top), tw)
    tidx_ref[...] = tidx.astype(jnp.int32)
    tw_ref[...] = tw / jnp.sum(tw, axis=-1, keepdims=True)
    hist_ref[...] = jnp.sum(jnp.where(work == neg, 1.0, 0.0), axis=0, keepdims=True)


_ROW_CHUNKS = D_MODEL // LANES


def _post(ctx_ops, lat_ops, mods_l, wts, *, tm, seq):
    ctx_tiles = ctx_ops[3].shape[0] // tm
    lat_tiles = lat_ops[3].shape[0] // tm
    nt = ctx_tiles + lat_tiles
    t = nt * tm
    per_seq = seq // tm
    row = lambda i: (i, 0)
    ctx = lambda i: (jnp.minimum(i, ctx_tiles - 1), 0)
    lat = lambda i: (jnp.maximum(i - ctx_tiles, 0), 0)
    const2 = lambda i: (0, 0)
    mod_idx = lambda i: (jnp.where(i < ctx_tiles, 0, 1 + (i - ctx_tiles) // per_seq), 0, 0)
    mix = A_HEADS * A_HEAD_DIM + LRU_WIDTH + MLA_HEADS * MLA_V
    widths = (A_HEADS * A_HEAD_DIM, LRU_WIDTH, MLA_HEADS * MLA_V, D_MODEL)
    in_specs = ([pl.BlockSpec((tm, w), ctx) for w in widths] + [pl.BlockSpec((tm, w), lat) for w in widths] + [
        pl.BlockSpec((None, 6, D_MODEL), mod_idx),
        pl.BlockSpec((mix, D_MODEL), const2),
        pl.BlockSpec((1, D_MODEL), const2),
        pl.BlockSpec((D_MODEL, LANES), const2),
        pl.BlockSpec((1, LANES), const2),
    ])
    return pl.pallas_call(
        functools.partial(_post_kernel, tm=tm, ctx_tiles=ctx_tiles),
        grid=(nt,),
        in_specs=in_specs,
        out_specs=[pl.BlockSpec((tm, D_MODEL), ctx), pl.BlockSpec((tm, D_MODEL), lat),
                   pl.BlockSpec((tm * _ROW_CHUNKS, LANES), row),
                   pl.BlockSpec((tm, LANES), row), pl.BlockSpec((tm, LANES), row),
                   pl.BlockSpec((None, 1, LANES), lambda i: (i, 0, 0))],
        out_shape=[jax.ShapeDtypeStruct((ctx_tiles * tm, D_MODEL), _F32),
                   jax.ShapeDtypeStruct((lat_tiles * tm, D_MODEL), _F32),
                   jax.ShapeDtypeStruct((t * _ROW_CHUNKS, LANES), _F32),
                   jax.ShapeDtypeStruct((t, LANES), jnp.int32), jax.ShapeDtypeStruct((t, LANES), _F32),
                   jax.ShapeDtypeStruct((nt, 1, LANES), _F32)],
        compiler_params=_cparams("arbitrary"),
        name="post",
    )(*ctx_ops, *lat_ops, mods_l, wts["w_out"], wts["norm2_g"], wts["router_w"], wts["router_b"])


MOE_TM = 256
_K_SHIFT = TOP_K.bit_length() - 1
assert 1 << _K_SHIFT == TOP_K


def _route(flat_e, cnt):
    a = flat_e.shape[0]
    t = a // TOP_K
    assert a & (a - 1) == 0 and t & (t - 1) == 0
    nts = a // MOE_TM + N_EXPERTS
    key = jnp.sort(flat_e * a + jnp.arange(a, dtype=jnp.int32))
    order = key & (a - 1)
    tiles = (cnt + MOE_TM - 1) // MOE_TM
    tile_end = jnp.cumsum(tiles)
    tile_start = tile_end - tiles
    grp_start = jnp.cumsum(cnt) - cnt
    n_tiles = tile_end[-1]
    tile_id = jnp.arange(nts, dtype=jnp.int32)
    tile_e = jnp.minimum(jnp.sum((tile_id[:, None] >= tile_end[None, :]).astype(jnp.int32), axis=1), N_EXPERTS - 1)
    tile_e = jnp.where(tile_id < n_tiles, tile_e, tile_e[jnp.maximum(n_tiles - 1, 0)])
    row = jnp.arange(MOE_TM, dtype=jnp.int32)[None, :]
    rank = (tile_id - tile_start[tile_e])[:, None] * MOE_TM + row
    valid = (rank < cnt[tile_e][:, None]) & (tile_id < n_tiles)[:, None]
    assign = order[jnp.clip(grp_start[tile_e][:, None] + rank, 0, a - 1)]
    real = (assign & (TOP_K - 1)) * t + lax.shift_right_logical(assign, _K_SHIFT)
    spare = TOP_K * t + (tile_id % 2)[:, None] * MOE_TM + row
    code = jnp.concatenate([TOP_K * t + MOE_TM + row, jnp.where(valid, real, spare)], axis=0)
    return tile_e.astype(jnp.int32), n_tiles.reshape(1).astype(jnp.int32), code.reshape(-1).astype(jnp.int32)


def _ffn_kernel(te_ref, nt_ref, code_ref, h2_hbm, w1_ref, b1_ref, w2_ref, b2_ref, yk_hbm,
                xb0, xb1, yb0, yb1, w1b, w2b, gsem, ssem, *, t):
    i = pl.program_id(0)
    nts = pl.num_programs(0)
    nt = nt_ref[0]
    xbuf = (xb0, xb1)
    ybuf = (yb0, yb1)
    rc = _ROW_CHUNKS
    spare0 = TOP_K * t * rc

    def tile_rows(ref, row):
        return ref.at[pl.ds(pl.multiple_of(row * rc, rc), rc)]

    def start_gather(tile, r, s, priority=0):
        tok = code_ref[(tile + 1) * MOE_TM + r] & (t - 1)
        pltpu.make_async_copy(tile_rows(h2_hbm, tok), tile_rows(xbuf[s], r), gsem.at[s]).start(priority=priority)

    def start_scatter(tile, r, s, priority=0):
        dst = code_ref[(tile + 1) * MOE_TM + r]
        pltpu.make_async_copy(tile_rows(ybuf[s], r), tile_rows(yk_hbm, dst), ssem.at[s]).start(priority=priority)

    def wait_gather(s):
        pltpu.make_async_copy(h2_hbm.at[pl.ds(0, MOE_TM * rc)], xbuf[s], gsem.at[s]).wait()

    def wait_scatter(s):
        pltpu.make_async_copy(ybuf[s], yk_hbm.at[pl.ds(0, MOE_TM * rc)], ssem.at[s]).wait()

    def loop_rows(fn):
        def body(r, carry):
            fn(r)
            return carry

        lax.fori_loop(0, MOE_TM, body, 0, unroll=8)

    @pl.when(i == 0)
    def _():
        yb0[...] = jnp.zeros_like(yb0)
        yb1[...] = jnp.zeros_like(yb1)
        pltpu.make_async_copy(yb0, yk_hbm.at[pl.ds(spare0, MOE_TM * rc)], ssem.at[0]).start()
        loop_rows(lambda r: start_gather(0, r, 0))

    def step(slot):
        other = 1 - slot
        x = jnp.concatenate([xbuf[slot][pl.ds(c, MOE_TM, stride=rc), :].astype(_BF) for c in range(rc)], axis=-1)
        nxt = jnp.minimum(i + 1, nts - 1)
        for r in range(MOE_TM):
            start_gather(nxt, r, other, priority=r % 2)
            start_scatter(i - 1, r, other, priority=(r + 1) % 2)
        gu = jnp.dot(x, w1b[...], preferred_element_type=_F32) + b1_ref[...]
        gt = jnp.minimum(gu[:, :D_FF], SWIGLU_LIMIT)
        up = jnp.clip(gu[:, D_FF:], -SWIGLU_LIMIT, SWIGLU_LIMIT)
        act = gt * jax.nn.sigmoid(SWIGLU_ALPHA * gt) * (up + 1.0)
        y = jnp.dot(act.astype(_BF), w2b[...], preferred_element_type=_F32) + b2_ref[...]
        wait_scatter(slot)
        for c in range(rc):
            ybuf[slot][pl.ds(c, MOE_TM, stride=rc), :] = y[:, c * LANES:(c + 1) * LANES]

    @pl.when(i < nt)
    def _():
        e_changed = jnp.logical_or(i == 0, te_ref[i] != te_ref[jnp.maximum(i - 1, 0)])

        @pl.when(e_changed)
        def _():
            w1b[...] = w1_ref[...].astype(_BF)
            w2b[...] = w2_ref[...].astype(_BF)

        for slot in range(2):
            @pl.when(i % 2 == slot)
            def _(slot=slot):
                wait_gather(slot)
                step(slot)

    @pl.when(i == nts - 1)
    def _():
        last = nt - 1
        for slot in range(2):
            @pl.when(last % 2 == slot)
            def _(slot=slot):
                loop_rows(lambda r: start_scatter(last, r, slot))
                wait_gather(1 - slot)

        wait_scatter(0)
        wait_scatter(1)


def _ffn(h2, tile_e, n_tiles, code, moe_w1, moe_w2, wts, layer):
    t = h2.shape[0] // _ROW_CHUNKS
    nts = tile_e.shape[0]
    by_expert = lambda i, te, nt, code: (te[i], 0, 0)
    by_layer_expert = lambda i, te, nt, code: (layer, te[i], 0, 0)
    grid_spec = pltpu.PrefetchScalarGridSpec(
        num_scalar_prefetch=3,
        grid=(nts,),
        in_specs=[
            pl.BlockSpec(memory_space=pl.ANY),
            pl.BlockSpec((None, None, D_MODEL, 2 * D_FF), by_layer_expert),
            pl.BlockSpec((None, 1, 2 * D_FF), by_expert),
            pl.BlockSpec((None, None, D_FF, D_MODEL), by_layer_expert),
            pl.BlockSpec((None, 1, D_MODEL), by_expert),
        ],
        out_specs=pl.BlockSpec(memory_space=pl.ANY),
        scratch_shapes=[
            pltpu.VMEM((MOE_TM * _ROW_CHUNKS, LANES), _F32),
            pltpu.VMEM((MOE_TM * _ROW_CHUNKS, LANES), _F32),
            pltpu.VMEM((MOE_TM * _ROW_CHUNKS, LANES), _F32),
            pltpu.VMEM((MOE_TM * _ROW_CHUNKS, LANES), _F32),
            pltpu.VMEM((D_MODEL, 2 * D_FF), _BF),
            pltpu.VMEM((D_FF, D_MODEL), _BF),
            pltpu.SemaphoreType.DMA((2,)),
            pltpu.SemaphoreType.DMA((2,)),
        ],
    )
    return pl.pallas_call(
        functools.partial(_ffn_kernel, t=t),
        grid_spec=grid_spec,
        out_shape=jax.ShapeDtypeStruct(((TOP_K * t + 2 * MOE_TM) * _ROW_CHUNKS, LANES), _F32),
        compiler_params=_cparams("arbitrary"),
        name="moe_ffn",
    )(tile_e, n_tiles, code, h2, moe_w1, wts["moe_b1"], moe_w2, wts["moe_b2"])


def _combine_kernel(x1c_ref, x1l_ref, tw_ref, mod_ref, y0_ref, y1_ref, y2_ref, y3_ref, oc_ref, ol_ref,
                    *, tm, ctx_tiles):
    y_refs = (y0_ref, y1_ref, y2_ref, y3_ref)

    def run(x1_ref, o_ref):
        tw = tw_ref[...]
        wk = [jnp.broadcast_to(tw[:, k:k + 1], (tm, LANES)) for k in range(TOP_K)]
        for c in range(_ROW_CHUNKS):
            cols = slice(c * LANES, (c + 1) * LANES)
            acc = wk[0] * y_refs[0][pl.ds(c, tm, stride=_ROW_CHUNKS), :]
            for k in range(1, TOP_K):
                acc = acc + wk[k] * y_refs[k][pl.ds(c, tm, stride=_ROW_CHUNKS), :]
            o_ref[:, cols] = x1_ref[:, cols] + mod_ref[5:6, cols] * acc

    i = pl.program_id(0)

    @pl.when(i < ctx_tiles)
    def _():
        run(x1c_ref, oc_ref)

    @pl.when(i >= ctx_tiles)
    def _():
        run(x1l_ref, ol_ref)


def _combine(x1c, x1l, tw, mods_l, yk, *, tm, seq):
    ctx_tiles = x1c.shape[0] // tm
    lat_tiles = x1l.shape[0] // tm
    nt = ctx_tiles + lat_tiles
    per_seq = seq // tm
    ctx = lambda i: (jnp.minimum(i, ctx_tiles - 1), 0)
    lat = lambda i: (jnp.maximum(i - ctx_tiles, 0), 0)
    mod_idx = lambda i: (jnp.where(i < ctx_tiles, 0, 1 + (i - ctx_tiles) // per_seq), 0, 0)
    plane = lambda k: (lambda i: (k * nt + i, 0))
    return pl.pallas_call(
        functools.partial(_combine_kernel, tm=tm, ctx_tiles=ctx_tiles),
        grid=(nt,),
        in_specs=[pl.BlockSpec((tm, D_MODEL), ctx), pl.BlockSpec((tm, D_MODEL), lat),
                  pl.BlockSpec((tm, LANES), lambda i: (i, 0)),
                  pl.BlockSpec((None, 6, D_MODEL), mod_idx)]
                 + [pl.BlockSpec((tm * _ROW_CHUNKS, LANES), plane(k)) for k in range(TOP_K)],
        out_specs=[pl.BlockSpec((tm, D_MODEL), ctx), pl.BlockSpec((tm, D_MODEL), lat)],
        out_shape=[jax.ShapeDtypeStruct(x1c.shape, _F32), jax.ShapeDtypeStruct(x1l.shape, _F32)],
        compiler_params=_cparams("arbitrary"),
        name="moe_combine",
    )(x1c, x1l, tw, mods_l, yk, yk, yk, yk)


def _pad_heads(w, heads, dim):
    lead = w.shape[:-1]
    w = w.reshape(lead + (heads, dim))
    w = jnp.pad(w, [(0, 0)] * len(lead) + [(0, 0), (0, LANES - dim)])
    return w.reshape(lead + (heads * LANES,))


def _pad_lanes(g, left=0):
    return jnp.pad(g, (left, LANES - left - g.shape[0])).reshape(1, LANES)


def _block_diag_halves(w):
    w4 = w.reshape(2, 4, 64, 64)
    eye = jnp.eye(4, dtype=w.dtype)
    return jnp.einsum("jaik,ab->jaibk", w4, eye).reshape(2, 256, 256)


def _layer_weights(l, p):
    w_in = p["w_in"][l]
    qa, ka, va, xb, yb, cq, ckv, kr = jnp.split(w_in, (512, 640, 768, 1280, 1792, 2176, 2432), axis=-1)
    kr128 = jnp.pad(kr, ((0, 0), (MLA_NOPE, LANES - MLA_NOPE - MLA_ROPE)))
    w_in_p = jnp.concatenate([_pad_heads(qa, A_HEADS, A_HEAD_DIM), _pad_heads(ka, A_KV_HEADS, A_HEAD_DIM), va, xb,
                              yb, cq, ckv, kr128], axis=-1).astype(_BF)
    ukv = p["mla_w_ukv"][l].reshape(KV_LORA, MLA_HEADS, MLA_NOPE + MLA_V)
    w_ukv = jnp.concatenate([_pad_heads(ukv[:, :, :MLA_NOPE].reshape(KV_LORA, -1), MLA_HEADS, MLA_NOPE),
                             ukv[:, :, MLA_NOPE:].reshape(KV_LORA, -1)], axis=-1).astype(_BF)
    wg = jnp.concatenate([_block_diag_halves(p["lru_wa"][l, 0]), _block_diag_halves(p["lru_wx"][l, 0]),
                          _block_diag_halves(p["lru_wa"][l, 1]), _block_diag_halves(p["lru_wx"][l, 1])],
                         axis=-1).astype(_BF)
    bg = jnp.stack([p["lru_ba"][l, 0], p["lru_bx"][l, 0], p["lru_ba"][l, 1], p["lru_bx"][l, 1]])
    return {
        "norm1_g": p["norm1_g"][l].reshape(1, -1),
        "norm2_g": p["norm2_g"][l].reshape(1, -1),
        "w_in": w_in_p,
        "a_q_g": _pad_lanes(p["a_q_g"][l]),
        "a_k_g": _pad_lanes(p["a_k_g"][l]),
        "mla_cq_g": p["mla_cq_g"][l].reshape(1, -1),
        "w_uq": _pad_heads(p["mla_w_uq"][l], MLA_HEADS, MLA_QK).astype(_BF),
        "mla_q_g": _pad_lanes(p["mla_q_g"][l]),
        "mla_ckv_g": p["mla_ckv_g"][l].reshape(1, -1),
        "w_ukv": w_ukv,
        "mla_k_g": _pad_lanes(p["mla_k_g"][l]),
        "conv_w": p["conv_w"][l],
        "conv_b": p["conv_b"][l].reshape(1, -1),
        "lru_wg": wg,
        "lru_bg": bg,
        "lru_lam": p["lru_lam"][l],
        "w_out": p["w_out"][l].astype(_BF),
        "router_w": jnp.pad(p["router_w"][l], ((0, 0), (0, LANES - N_EXPERTS))),
        "router_b": jnp.pad(p["router_b"][l], (0, LANES - N_EXPERTS), constant_values=-1e30).reshape(1, LANES),
        "moe_b1": p["moe_b1"][l].reshape(N_EXPERTS, 1, 2 * D_FF),
        "moe_b2": p["moe_b2"][l].reshape(N_EXPERTS, 1, D_MODEL),
    }


def _rope_tables(n, rot_dim, lane0):
    rows = n // GRID_W
    r = jnp.repeat(jnp.arange(rows, dtype=_F32), GRID_W)
    col = jnp.tile(jnp.arange(GRID_W, dtype=_F32), rows)
    n_freq = rot_dim // 4
    inv = ROPE_THETA ** (-jnp.arange(n_freq, dtype=_F32) / n_freq)
    ang = jnp.concatenate([r[:, None] * inv, col[:, None] * inv], axis=-1)
    cos = jnp.repeat(jnp.cos(ang), 2, axis=-1)
    sin = jnp.repeat(jnp.sin(ang), 2, axis=-1) * jnp.tile(jnp.array([-1.0, 1.0], _F32), rot_dim // 2)
    pad = ((0, 0), (lane0, LANES - lane0 - rot_dim))
    return jnp.pad(cos, pad, constant_values=1.0), jnp.pad(sin, pad)


def kernel(x_prompt, x_sample, cache_attn_k, cache_attn_v, state_lru, cache_mla_ckv, cache_mla_krope, c, c_ctx,
           ada_w, ada_b, norm1_g, norm2_g, w_in, a_q_g, a_k_g, conv_w, conv_b, lru_wa, lru_ba, lru_wx, lru_bx,
           lru_lam, mla_cq_g, mla_w_uq, mla_q_g, mla_ckv_g, mla_w_ukv, mla_k_g, w_out, router_w, router_b,
           moe_w1, moe_b1, moe_w2, moe_b2):
    p = dict(norm1_g=norm1_g, norm2_g=norm2_g, w_in=w_in, a_q_g=a_q_g, a_k_g=a_k_g, conv_w=conv_w, conv_b=conv_b,
             lru_wa=lru_wa, lru_ba=lru_ba, lru_wx=lru_wx, lru_bx=lru_bx, lru_lam=lru_lam, mla_cq_g=mla_cq_g,
             mla_w_uq=mla_w_uq, mla_q_g=mla_q_g, mla_ckv_g=mla_ckv_g, mla_w_ukv=mla_w_ukv, mla_k_g=mla_k_g,
             w_out=w_out, router_w=router_w, router_b=router_b, moe_w1=moe_w1, moe_b1=moe_b1, moe_w2=moe_w2,
             moe_b2=moe_b2)
    batch, seq, _ = x_prompt.shape
    dec_batch, dec_seq, _ = x_sample.shape
    past = cache_attn_k.shape[2]

    cvec = jnp.concatenate([c_ctx[None, :], c, jnp.zeros((N_MODS - 1 - dec_batch, D_MODEL), _F32)], axis=0)
    mods = _ada_mods(cvec, ada_w, ada_b)
    cos_a, sin_a = _rope_tables(dec_seq, A_HEAD_DIM, 0)
    cos_c, sin_c = _rope_tables(dec_seq, MLA_ROPE, MLA_NOPE)
    tabs = (cos_a, sin_a, cos_c, sin_c)

    yp = x_prompt.reshape(batch * seq, D_MODEL)
    ys = x_sample.reshape(dec_batch * dec_seq, D_MODEL)
    zero_state = jnp.zeros((batch, 2, LRU_WIDTH), _F32)
    ks_l, vs_l, hs_l, ckv_l, kr_l = [], [], [], [], []
    tm = 256
    n_tok = batch * seq + dec_batch * dec_seq
    for l in range(DEPTH):
        wts = _layer_weights(l, p)
        mods_l = mods[l]

        qa, kab, kaf, vab, vaf, xb, yb, qc, ckvn, kr128, kr = _proj(yp, mods_l, wts, None, rope=False, tm=tm, seq=seq)
        kc, vc = _mlakv(ckvn, kr128, wts, None, rope=False, tm=tm, seq=seq)
        ob, st = _lru(xb, yb, zero_state, wts, n=seq)
        oa = _attn(qa, kab, vab, None, None, heads=A_HEADS, kv_heads=A_KV_HEADS, dv=A_HEAD_DIM, n=seq, tq=seq,
                   name="attn_a_ctx")
        oc = _attn(qc, kc, vc, None, None, heads=MLA_HEADS, kv_heads=MLA_HEADS, dv=MLA_V, n=seq, tq=seq,
                   name="attn_c_ctx")
        ctx_ops = (oa, ob, oc, yp)
        ks_l.append(kaf.reshape(batch, seq, A_KV_HEADS, A_HEAD_DIM))
        vs_l.append(vaf.reshape(batch, seq, A_KV_HEADS, A_HEAD_DIM))
        hs_l.append(st)
        ckv_l.append(ckvn.reshape(batch, seq, KV_LORA))
        kr_l.append(kr.reshape(batch, seq, MLA_ROPE))

        qa, kab, _, vab, _, xb, yb, qc, ckvn, kr128, _ = _proj(ys, mods_l, wts, tabs, rope=True, tm=tm, seq=dec_seq)
        kc, vc = _mlakv(ckvn, kr128, wts, tabs, rope=True, tm=tm, seq=dec_seq)
        c_kr128 = jnp.pad(cache_mla_krope[:, l].reshape(dec_batch * past, MLA_ROPE),
                          ((0, 0), (MLA_NOPE, LANES - MLA_NOPE - MLA_ROPE)))
        kx, vx = _mlakv(cache_mla_ckv[:, l].reshape(dec_batch * past, KV_LORA), c_kr128, wts, None, rope=False,
                        tm=tm, seq=past)
        ob, _ = _lru(xb, yb, state_lru[:, l], wts, n=dec_seq)
        c_k = _pad_heads(cache_attn_k[:, l].reshape(dec_batch * past, A_KV_HEADS * A_HEAD_DIM), A_KV_HEADS,
                         A_HEAD_DIM).astype(_BF)
        c_v = cache_attn_v[:, l].reshape(dec_batch * past, A_KV_HEADS * A_HEAD_DIM).astype(_BF)
        oa = _attn(qa, kab, vab, c_k, c_v, heads=A_HEADS, kv_heads=A_KV_HEADS, dv=A_HEAD_DIM, n=dec_seq, tq=256,
                   name="attn_a_lat")
        oc = _attn(qc, kc, vc, kx, vx, heads=MLA_HEADS, kv_heads=MLA_HEADS, dv=MLA_V, n=dec_seq, tq=256,
                   name="attn_c_lat")
        x1c, x1l, h2, tidx, tw, hist = _post(ctx_ops, (oa, ob, oc, ys), mods_l, wts, tm=tm, seq=dec_seq)
        flat_e = tidx[:, :TOP_K].reshape(n_tok * TOP_K)
        cnt = jnp.sum(hist, axis=(0, 1))[:N_EXPERTS].astype(jnp.int32)
        tile_e, n_tiles, code = _route(flat_e, cnt)
        yk = _ffn(h2, tile_e, n_tiles, code, moe_w1, moe_w2, wts, l)
        yp, ys = _combine(x1c, x1l, tw, mods_l, yk, tm=tm, seq=dec_seq)

    return (yp.reshape(batch, seq, D_MODEL), ys.reshape(dec_batch, dec_seq, D_MODEL),
            jnp.stack(ks_l, axis=1), jnp.stack(vs_l, axis=1), jnp.stack(hs_l, axis=1),
            jnp.stack(ckv_l, axis=1), jnp.stack(kr_l, axis=1))
```

```python
import functools

import jax
import jax.numpy as jnp
from jax import lax
from jax.experimental import pallas as pl
from jax.experimental.pallas import tpu as pltpu

D_MODEL = 1024
DEPTH = 2
GRID_W = 64
ROPE_THETA = 10000.0
EPS = 1e-6
A_HEADS = 8
A_KV_HEADS = 2
A_HEAD_DIM = 64
LRU_WIDTH = 512
LRU_BLOCKS = 8
CONV_W = 4
LRU_C = 8.0
MLA_HEADS = 8
MLA_NOPE = 64
MLA_ROPE = 32
MLA_QK = MLA_NOPE + MLA_ROPE
MLA_V = 64
Q_LORA = 384
KV_LORA = 256
N_EXPERTS = 32
TOP_K = 4
D_FF = 1024
SWIGLU_LIMIT = 7.0
SWIGLU_ALPHA = 1.702

LANES = 128
N_MODS = 16
VMEM_LIMIT = 56 * 1024 * 1024

_QA0 = 0
_KA0 = _QA0 + A_HEADS * LANES
_VA0 = _KA0 + A_KV_HEADS * LANES
_XB0 = _VA0 + A_KV_HEADS * A_HEAD_DIM
_YB0 = _XB0 + LRU_WIDTH
_CQ0 = _YB0 + LRU_WIDTH
_CKV0 = _CQ0 + Q_LORA
_KR0 = _CKV0 + KV_LORA
_NP = _KR0 + LANES

_BF = jnp.bfloat16
_F32 = jnp.float32


def _cparams(*sem):
    return pltpu.CompilerParams(dimension_semantics=sem, vmem_limit_bytes=VMEM_LIMIT)


def _split_dot(a, b):
    a_hi = a.astype(_BF)
    a_lo = (a - a_hi.astype(_F32)).astype(_BF)
    b_hi = b.astype(_BF)
    b_lo = (b - b_hi.astype(_F32)).astype(_BF)
    d = functools.partial(jnp.dot, preferred_element_type=_F32)
    return d(a_hi, b_hi) + (d(a_hi, b_lo) + d(a_lo, b_hi))


def _rope_rotate(x, cos, sin_signed):
    lane = lax.broadcasted_iota(jnp.int32, x.shape, 1)
    swapped = jnp.where((lane & 1) == 0, pltpu.roll(x, LANES - 1, 1), pltpu.roll(x, 1, 1))
    return x * cos + swapped * sin_signed


def _head_rms(xh, denom):
    return xh * lax.rsqrt(jnp.sum(xh * xh, axis=-1, keepdims=True) * (1.0 / denom) + EPS)


def _mods_kernel(c_ref, w_ref, b_ref, o_ref):
    c = c_ref[...]
    s = c * jax.nn.sigmoid(c)
    o_ref[...] = _split_dot(s, w_ref[...]) + b_ref[...]


def _ada_mods(cvec, ada_w, ada_b):
    tn = 1536
    n_out = ada_w.shape[-1]
    out = pl.pallas_call(
        _mods_kernel,
        grid=(DEPTH, n_out // tn),
        in_specs=[
            pl.BlockSpec((N_MODS, D_MODEL), lambda l, j: (0, 0)),
            pl.BlockSpec((None, D_MODEL, tn), lambda l, j: (l, 0, j)),
            pl.BlockSpec((None, 1, tn), lambda l, j: (l, 0, j)),
        ],
        out_specs=pl.BlockSpec((None, N_MODS, tn), lambda l, j: (l, 0, j)),
        out_shape=jax.ShapeDtypeStruct((DEPTH, N_MODS, n_out), _F32),
        compiler_params=_cparams("arbitrary", "arbitrary"),
        name="ada_mods",
    )(cvec, ada_w, ada_b.reshape(DEPTH, 1, n_out))
    return out.reshape(DEPTH, N_MODS, 6, D_MODEL)


def _proj_kernel(*refs, rope):
    if rope:
        (x_ref, mod_ref, n1g_ref, win_ref, qg_ref, kg_ref, cqg_ref, wuq_ref, mqg_ref, ckvg_ref,
         cosa_ref, sina_ref, cosc_ref, sinc_ref,
         qa_ref, kab_ref, kaf_ref, vab_ref, vaf_ref, xb_ref, yb_ref, qc_ref, ckvn_ref, kr128_ref, kr_ref) = refs
    else:
        (x_ref, mod_ref, n1g_ref, win_ref, qg_ref, kg_ref, cqg_ref, wuq_ref, mqg_ref, ckvg_ref,
         qa_ref, kab_ref, kaf_ref, vab_ref, vaf_ref, xb_ref, yb_ref, qc_ref, ckvn_ref, kr128_ref, kr_ref) = refs
    x = x_ref[...]
    xn = x * lax.rsqrt(jnp.mean(x * x, axis=-1, keepdims=True) + EPS) * n1g_ref[...]
    h = xn * (1.0 + mod_ref[1:2, :]) + mod_ref[0:1, :]
    z = jnp.dot(h.astype(_BF), win_ref[...], preferred_element_type=_F32)

    for hd in range(A_HEADS):
        q = _head_rms(z[:, _QA0 + hd * LANES:_QA0 + (hd + 1) * LANES], A_HEAD_DIM) * qg_ref[...]
        if rope:
            q = _rope_rotate(q, cosa_ref[...], sina_ref[...])
        qa_ref[:, hd * LANES:(hd + 1) * LANES] = (q * (A_HEAD_DIM ** -0.5)).astype(_BF)
    k_heads = []
    for hd in range(A_KV_HEADS):
        k = _head_rms(z[:, _KA0 + hd * LANES:_KA0 + (hd + 1) * LANES], A_HEAD_DIM) * kg_ref[...]
        k_heads.append(k[:, :A_HEAD_DIM])
        if rope:
            k = _rope_rotate(k, cosa_ref[...], sina_ref[...])
        kab_ref[:, hd * LANES:(hd + 1) * LANES] = k.astype(_BF)
    kaf_ref[...] = jnp.concatenate(k_heads, axis=-1)
    va = z[:, _VA0:_VA0 + A_KV_HEADS * A_HEAD_DIM]
    vaf_ref[...] = va
    vab_ref[...] = va.astype(_BF)

    xb_ref[...] = z[:, _XB0:_XB0 + LRU_WIDTH]
    yb_ref[...] = z[:, _YB0:_YB0 + LRU_WIDTH]

    cq = z[:, _CQ0:_CQ0 + Q_LORA]
    cqn = cq * lax.rsqrt(jnp.mean(cq * cq, axis=-1, keepdims=True) + EPS) * cqg_ref[...]
    zq = jnp.dot(cqn.astype(_BF), wuq_ref[...], preferred_element_type=_F32)
    for hd in range(MLA_HEADS):
        q = _head_rms(zq[:, hd * LANES:(hd + 1) * LANES], MLA_QK) * mqg_ref[...]
        if rope:
            q = _rope_rotate(q, cosc_ref[...], sinc_ref[...])
        qc_ref[:, hd * LANES:(hd + 1) * LANES] = (q * (MLA_QK ** -0.5)).astype(_BF)
    ckv = z[:, _CKV0:_CKV0 + KV_LORA]
    ckvn_ref[...] = ckv * lax.rsqrt(jnp.mean(ckv * ckv, axis=-1, keepdims=True) + EPS) * ckvg_ref[...]
    kr128 = z[:, _KR0:_KR0 + LANES]
    kr128_ref[...] = kr128
    kr_ref[...] = kr128[:, MLA_NOPE:MLA_NOPE + MLA_ROPE]


def _proj(x, mods_l, wts, tabs, *, rope, tm, seq):
    t = x.shape[0]
    nt = t // tm
    per_seq = seq // tm

    def row(i):
        return (i, 0)

    def const2(i):
        return (0, 0)

    if rope:
        def mod_idx(i):
            return (1 + i // per_seq, 0, 0)

        def tab_idx(i):
            return (i % per_seq, 0)
    else:
        def mod_idx(i):
            return (0, 0, 0)

    in_specs = [
        pl.BlockSpec((tm, D_MODEL), row),
        pl.BlockSpec((None, 6, D_MODEL), mod_idx),
        pl.BlockSpec((1, D_MODEL), const2),
        pl.BlockSpec((D_MODEL, _NP), const2),
        pl.BlockSpec((1, LANES), const2),
        pl.BlockSpec((1, LANES), const2),
        pl.BlockSpec((1, Q_LORA), const2),
        pl.BlockSpec((Q_LORA, MLA_HEADS * LANES), const2),
        pl.BlockSpec((1, LANES), const2),
        pl.BlockSpec((1, KV_LORA), const2),
    ]
    args = [x, mods_l, wts["norm1_g"], wts["w_in"], wts["a_q_g"], wts["a_k_g"], wts["mla_cq_g"], wts["w_uq"],
            wts["mla_q_g"], wts["mla_ckv_g"]]
    if rope:
        in_specs += [pl.BlockSpec((tm, LANES), tab_idx)] * 4
        args += list(tabs)
    widths = [(A_HEADS * LANES, _BF), (A_KV_HEADS * LANES, _BF), (A_KV_HEADS * A_HEAD_DIM, _F32),
              (A_KV_HEADS * A_HEAD_DIM, _BF), (A_KV_HEADS * A_HEAD_DIM, _F32), (LRU_WIDTH, _F32), (LRU_WIDTH, _F32),
              (MLA_HEADS * LANES, _BF), (KV_LORA, _F32), (LANES, _F32), (MLA_ROPE, _F32)]
    out_specs = [pl.BlockSpec((tm, w), row) for w, _ in widths]
    out_shape = [jax.ShapeDtypeStruct((t, w), dt) for w, dt in widths]
    return pl.pallas_call(
        functools.partial(_proj_kernel, rope=rope),
        grid=(nt,),
        in_specs=in_specs,
        out_specs=out_specs,
        out_shape=out_shape,
        compiler_params=_cparams("arbitrary"),
        name="proj_rope" if rope else "proj",
    )(*args)


def _mlakv_kernel(*refs, rope):
    if rope:
        ckvn_ref, kr128_ref, wukv_ref, kg_ref, cos_ref, sin_ref, k_ref, v_ref = refs
    else:
        ckvn_ref, kr128_ref, wukv_ref, kg_ref, k_ref, v_ref = refs
    kv = jnp.dot(ckvn_ref[...].astype(_BF), wukv_ref[...], preferred_element_type=_F32)
    kr128 = kr128_ref[...]
    for hd in range(MLA_HEADS):
        k = _head_rms(kv[:, hd * LANES:(hd + 1) * LANES] + kr128, MLA_QK) * kg_ref[...]
        if rope:
            k = _rope_rotate(k, cos_ref[...], sin_ref[...])
        k_ref[:, hd * LANES:(hd + 1) * LANES] = k.astype(_BF)
    v_ref[...] = kv[:, MLA_HEADS * LANES:].astype(_BF)


def _mlakv(ckvn, kr128, wts, tabs, *, rope, tm, seq):
    t = ckvn.shape[0]
    per_seq = seq // tm
    in_specs = [
        pl.BlockSpec((tm, KV_LORA), lambda i: (i, 0)),
        pl.BlockSpec((tm, LANES), lambda i: (i, 0)),
        pl.BlockSpec((KV_LORA, MLA_HEADS * (LANES + MLA_V)), lambda i: (0, 0)),
        pl.BlockSpec((1, LANES), lambda i: (0, 0)),
    ]
    args = [ckvn, kr128, wts["w_ukv"], wts["mla_k_g"]]
    if rope:
        in_specs += [pl.BlockSpec((tm, LANES), lambda i: (i % per_seq, 0))] * 2
        args += [tabs[2], tabs[3]]
    return pl.pallas_call(
        functools.partial(_mlakv_kernel, rope=rope),
        grid=(t // tm,),
        in_specs=in_specs,
        out_specs=[pl.BlockSpec((tm, MLA_HEADS * LANES), lambda i: (i, 0)),
                   pl.BlockSpec((tm, MLA_HEADS * MLA_V), lambda i: (i, 0))],
        out_shape=[jax.ShapeDtypeStruct((t, MLA_HEADS * LANES), _BF),
                   jax.ShapeDtypeStruct((t, MLA_HEADS * MLA_V), _BF)],
        compiler_params=_cparams("arbitrary"),
        name="mlakv_rope" if rope else "mlakv",
    )(*args)


_CHUNK = 8


def _lru_kernel(xb_ref, yb_ref, h0_ref, cw_ref, cb_ref, wg_ref, bg_ref, lam_ref,
                ob_ref, st_ref, af_ref, bf_ref, ab_ref, bb_ref, h_ref, *, n):
    x = xb_ref[...]
    rows = lax.broadcasted_iota(jnp.int32, x.shape, 0)
    xc = cb_ref[...] + x * cw_ref[2:3, :]
    xc = xc + jnp.where(rows >= 2, pltpu.roll(x, 2, 0), 0.0) * cw_ref[0:1, :]
    xc = xc + jnp.where(rows >= 1, pltpu.roll(x, 1, 0), 0.0) * cw_ref[1:2, :]
    xc = xc + jnp.where(rows < n - 1, pltpu.roll(x, n - 1, 0), 0.0) * cw_ref[3:4, :]

    xcb = xc.astype(_BF)
    half = LRU_WIDTH // 2
    pre = [jnp.dot(xcb[:, j * half:(j + 1) * half], wg_ref[j], preferred_element_type=_F32) for j in range(2)]
    for d, (a_ref, b_ref) in enumerate(((af_ref, bf_ref), (ab_ref, bb_ref))):
        pa = jnp.concatenate([pre[0][:, (2 * d) * half:(2 * d + 1) * half],
                              pre[1][:, (2 * d) * half:(2 * d + 1) * half]], axis=-1)
        px = jnp.concatenate([pre[0][:, (2 * d + 1) * half:(2 * d + 2) * half],
                              pre[1][:, (2 * d + 1) * half:(2 * d + 2) * half]], axis=-1)
        r = jax.nn.sigmoid(pa + bg_ref[2 * d:2 * d + 1, :])
        i = jax.nn.sigmoid(px + bg_ref[2 * d + 1:2 * d + 2, :])
        nlam = -lam_ref[d:d + 1, :]
        softplus = jnp.maximum(nlam, 0.0) + jnp.log1p(jnp.exp(-jnp.abs(nlam)))
        log_a = (-LRU_C) * r * softplus
        a = jnp.exp(log_a)
        a_ref[...] = a
        b_ref[...] = jnp.sqrt(-jnp.tanh(log_a) * (a * a + 1.0)) * i * xc

    nchunks = n // _CHUNK
    crow = lax.broadcasted_iota(jnp.int32, (_CHUNK, LRU_WIDTH), 0)

    def chunk_scan(a, b, carry, reverse):
        for d in (1, 2, 4):
            if reverse:
                keep = crow < _CHUNK - d
                a_s = pltpu.roll(a, _CHUNK - d, 0)
                b_s = pltpu.roll(b, _CHUNK - d, 0)
            else:
                keep = crow >= d
                a_s = pltpu.roll(a, d, 0)
                b_s = pltpu.roll(b, d, 0)
            b = jnp.where(keep, a * b_s + b, b)
            a = jnp.where(keep, a * a_s, a)
        return a * carry + b

    def body(ci, carry):
        cf, cb = carry
        off_f = pl.multiple_of(ci * _CHUNK, _CHUNK)
        off_b = pl.multiple_of((nchunks - 1 - ci) * _CHUNK, _CHUNK)
        hf = chunk_scan(af_ref[pl.ds(off_f, _CHUNK), :], bf_ref[pl.ds(off_f, _CHUNK), :], cf, False)
        hb = chunk_scan(ab_ref[pl.ds(off_b, _CHUNK), :], bb_ref[pl.ds(off_b, _CHUNK), :], cb, True)
        bf_ref[pl.ds(off_f, _CHUNK), :] = hf
        bb_ref[pl.ds(off_b, _CHUNK), :] = hb
        return hf[_CHUNK - 1:_CHUNK, :], hb[0:1, :]

    cf, cb = lax.fori_loop(0, nchunks, body, (h0_ref[0:1, :], h0_ref[1:2, :]))
    st_ref[0:1, :] = cf
    st_ref[1:2, :] = cb
    ob_ref[...] = ((bf_ref[...] + bb_ref[...]) * jax.nn.gelu(yb_ref[...])).astype(_BF)


def _lru(xb, yb, h0, wts, *, n):
    nb = xb.shape[0] // n
    seq = lambda b: (b, 0)
    const2 = lambda b: (0, 0)
    return pl.pallas_call(
        functools.partial(_lru_kernel, n=n),
        grid=(nb,),
        in_specs=[
            pl.BlockSpec((n, LRU_WIDTH), seq),
            pl.BlockSpec((n, LRU_WIDTH), seq),
            pl.BlockSpec((None, 2, LRU_WIDTH), lambda b: (b, 0, 0)),
            pl.BlockSpec((CONV_W, LRU_WIDTH), const2),
            pl.BlockSpec((1, LRU_WIDTH), const2),
            pl.BlockSpec((2, LRU_WIDTH // 2, 2 * LRU_WIDTH), lambda b: (0, 0, 0)),
            pl.BlockSpec((4, LRU_WIDTH), const2),
            pl.BlockSpec((2, LRU_WIDTH), const2),
        ],
        out_specs=[pl.BlockSpec((n, LRU_WIDTH), seq),
                   pl.BlockSpec((None, 2, LRU_WIDTH), lambda b: (b, 0, 0))],
        out_shape=[jax.ShapeDtypeStruct((nb * n, LRU_WIDTH), _BF),
                   jax.ShapeDtypeStruct((nb, 2, LRU_WIDTH), _F32)],
        scratch_shapes=[pltpu.VMEM((n, LRU_WIDTH), _F32) for _ in range(5)],
        compiler_params=_cparams("arbitrary"),
        name=f"lru_{n}",
    )(xb, yb, h0, wts["conv_w"], wts["conv_b"], wts["lru_wg"], wts["lru_bg"], wts["lru_lam"])


def _attn_kernel(*refs, heads, kv_heads, dv, cached):
    if cached:
        q_ref, k_ref, v_ref, kc_ref, vc_ref, o_ref = refs
    else:
        q_ref, k_ref, v_ref, o_ref = refs
    group = heads // kv_heads
    contract_last = (((1,), (1,)), ((), ()))
    outs = []
    for hd in range(heads):
        g = hd // group
        q = q_ref[:, hd * LANES:(hd + 1) * LANES]
        s = lax.dot_general(q, k_ref[:, g * LANES:(g + 1) * LANES], contract_last, preferred_element_type=_F32)
        m = jnp.max(s, axis=-1, keepdims=True)
        if cached:
            sc = lax.dot_general(q, kc_ref[:, g * LANES:(g + 1) * LANES], contract_last,
                                 preferred_element_type=_F32)
            m = jnp.maximum(m, jnp.max(sc, axis=-1, keepdims=True))
        p = jnp.exp(s - m)
        den = jnp.sum(p, axis=-1, keepdims=True)
        o = jnp.dot(p.astype(_BF), v_ref[:, g * dv:(g + 1) * dv], preferred_element_type=_F32)
        if cached:
            pc = jnp.exp(sc - m)
            den = den + jnp.sum(pc, axis=-1, keepdims=True)
            o = o + jnp.dot(pc.astype(_BF), vc_ref[:, g * dv:(g + 1) * dv], preferred_element_type=_F32)
        outs.append(o / den)
    o_ref[...] = jnp.concatenate(outs, axis=-1).astype(_BF)


def _attn(q, k, v, kc, vc, *, heads, kv_heads, dv, n, tq, name):
    t = q.shape[0]
    nb = t // n
    nq = n // tq
    cached = kc is not None
    in_specs = [
        pl.BlockSpec((tq, heads * LANES), lambda b, i: (b * nq + i, 0)),
        pl.BlockSpec((n, kv_heads * LANES), lambda b, i: (b, 0)),
        pl.BlockSpec((n, kv_heads * dv), lambda b, i: (b, 0)),
    ]
    args = [q, k, v]
    if cached:
        nc = kc.shape[0] // nb
        in_specs += [pl.BlockSpec((nc, kv_heads * LANES), lambda b, i: (b, 0)),
                     pl.BlockSpec((nc, kv_heads * dv), lambda b, i: (b, 0))]
        args += [kc, vc]
    return pl.pallas_call(
        functools.partial(_attn_kernel, heads=heads, kv_heads=kv_heads, dv=dv, cached=cached),
        grid=(nb, nq),
        in_specs=in_specs,
        out_specs=pl.BlockSpec((tq, heads * dv), lambda b, i: (b * nq + i, 0)),
        out_shape=jax.ShapeDtypeStruct((t, heads * dv), _BF),
        compiler_params=_cparams("arbitrary", "arbitrary"),
        name=name,
    )(*args)


def _post_kernel(oac_ref, obc_ref, occ_ref, xc_ref, oal_ref, obl_ref, ocl_ref, xl_ref,
                 mod_ref, wout_ref, n2g_ref, rw_ref, rb_ref,
                 x1c_ref, x1l_ref, h2_ref, tidx_ref, tw_ref, hist_ref, *, tm, ctx_tiles):
    i = pl.program_id(0)

    @pl.when(i < ctx_tiles)
    def _():
        _post_block(oac_ref, obc_ref, occ_ref, xc_ref, mod_ref, wout_ref, n2g_ref, rw_ref, rb_ref,
                    x1c_ref, h2_ref, tidx_ref, tw_ref, hist_ref, tm)

    @pl.when(i >= ctx_tiles)
    def _():
        _post_block(oal_ref, obl_ref, ocl_ref, xl_ref, mod_ref, wout_ref, n2g_ref, rw_ref, rb_ref,
                    x1l_ref, h2_ref, tidx_ref, tw_ref, hist_ref, tm)


def _post_block(oa_ref, ob_ref, oc_ref, x_ref, mod_ref, wout_ref, n2g_ref, rw_ref, rb_ref,
                x1_ref, h2_ref, tidx_ref, tw_ref, hist_ref, tm):
    w0 = A_HEADS * A_HEAD_DIM
    w1 = w0 + LRU_WIDTH
    o = jnp.dot(oa_ref[...], wout_ref[0:w0, :], preferred_element_type=_F32)
    o = o + jnp.dot(ob_ref[...], wout_ref[w0:w1, :], preferred_element_type=_F32)
    o = o + jnp.dot(oc_ref[...], wout_ref[w1:, :], preferred_element_type=_F32)
    x1 = x_ref[...] + mod_ref[2:3, :] * o
    x1_ref[...] = x1
    xn = x1 * lax.rsqrt(jnp.mean(x1 * x1, axis=-1, keepdims=True) + EPS) * n2g_ref[...]
    h2 = xn * (1.0 + mod_ref[4:5, :]) + mod_ref[3:4, :]
    for c in range(_ROW_CHUNKS):
        h2_ref[pl.ds(c, tm, stride=_ROW_CHUNKS), :] = h2[:, c * LANES:(c + 1) * LANES]

    logits = _split_dot(h2, rw_ref[...]) + rb_ref[...]
    lane = lax.broadcasted_iota(jnp.int32, logits.shape, 1).astype(_F32)
    neg = jnp.float32(-jnp.inf)
    work = logits
    top = None
    tidx = jnp.zeros(logits.shape, _F32)
    tw = jnp.zeros(logits.shape, _F32)
    for k in range(TOP_K):
        m = jnp.max(work, axis=-1, keepdims=True)
        if top is None:
            top = m
        first = jnp.min(jnp.where(work == m, lane, float(LANES)), axis=-1, keepdims=True)
        work = jnp.where(lane == first, neg, work)
        tidx = jnp.where(lane == float(k), first, tidx)
        tw = jnp.where(lane == float(k), jnp.exp(m - top), tw)
    tidx_ref[...] = tidx.astype(jnp.int32)
    tw_ref[...] = tw / jnp.sum(tw, axis=-1, keepdims=True)
    hist_ref[...] = jnp.sum(jnp.where(work == neg, 1.0, 0.0), axis=0, keepdims=True)


_ROW_CHUNKS = D_MODEL // LANES


def _post(ctx_ops, lat_ops, mods_l, wts, *, tm, seq):
    ctx_tiles = ctx_ops[3].shape[0] // tm
    lat_tiles = lat_ops[3].shape[0] // tm
    nt = ctx_tiles + lat_tiles
    t = nt * tm
    per_seq = seq // tm
    row = lambda i: (i, 0)
    ctx = lambda i: (jnp.minimum(i, ctx_tiles - 1), 0)
    lat = lambda i: (jnp.maximum(i - ctx_tiles, 0), 0)
    const2 = lambda i: (0, 0)
    mod_idx = lambda i: (jnp.where(i < ctx_tiles, 0, 1 + (i - ctx_tiles) // per_seq), 0, 0)
    mix = A_HEADS * A_HEAD_DIM + LRU_WIDTH + MLA_HEADS * MLA_V
    widths = (A_HEADS * A_HEAD_DIM, LRU_WIDTH, MLA_HEADS * MLA_V, D_MODEL)
    in_specs = ([pl.BlockSpec((tm, w), ctx) for w in widths] + [pl.BlockSpec((tm, w), lat) for w in widths] + [
        pl.BlockSpec((None, 6, D_MODEL), mod_idx),
        pl.BlockSpec((mix, D_MODEL), const2),
        pl.BlockSpec((1, D_MODEL), const2),
        pl.BlockSpec((D_MODEL, LANES), const2),
        pl.BlockSpec((1, LANES), const2),
    ])
    return pl.pallas_call(
        functools.partial(_post_kernel, tm=tm, ctx_tiles=ctx_tiles),
        grid=(nt,),
        in_specs=in_specs,
        out_specs=[pl.BlockSpec((tm, D_MODEL), ctx), pl.BlockSpec((tm, D_MODEL), lat),
                   pl.BlockSpec((tm * _ROW_CHUNKS, LANES), row),
                   pl.BlockSpec((tm, LANES), row), pl.BlockSpec((tm, LANES), row),
                   pl.BlockSpec((None, 1, LANES), lambda i: (i, 0, 0))],
        out_shape=[jax.ShapeDtypeStruct((ctx_tiles * tm, D_MODEL), _F32),
                   jax.ShapeDtypeStruct((lat_tiles * tm, D_MODEL), _F32),
                   jax.ShapeDtypeStruct((t * _ROW_CHUNKS, LANES), _F32),
                   jax.ShapeDtypeStruct((t, LANES), jnp.int32), jax.ShapeDtypeStruct((t, LANES), _F32),
                   jax.ShapeDtypeStruct((nt, 1, LANES), _F32)],
        compiler_params=_cparams("arbitrary"),
        name="post",
    )(*ctx_ops, *lat_ops, mods_l, wts["w_out"], wts["norm2_g"], wts["router_w"], wts["router_b"])


MOE_TM = 512
_K_SHIFT = TOP_K.bit_length() - 1
assert 1 << _K_SHIFT == TOP_K


def _route(flat_e, cnt):
    a = flat_e.shape[0]
    t = a // TOP_K
    assert a & (a - 1) == 0 and t & (t - 1) == 0
    nts = a // MOE_TM + N_EXPERTS
    key = jnp.sort(flat_e * a + jnp.arange(a, dtype=jnp.int32))
    order = key & (a - 1)
    tiles = (cnt + MOE_TM - 1) // MOE_TM
    tile_end = jnp.cumsum(tiles)
    tile_start = tile_end - tiles
    grp_start = jnp.cumsum(cnt) - cnt
    n_tiles = tile_end[-1]
    tile_id = jnp.arange(nts, dtype=jnp.int32)
    tile_e = jnp.minimum(jnp.sum((tile_id[:, None] >= tile_end[None, :]).astype(jnp.int32), axis=1), N_EXPERTS - 1)
    tile_e = jnp.where(tile_id < n_tiles, tile_e, tile_e[jnp.maximum(n_tiles - 1, 0)])
    row = jnp.arange(MOE_TM, dtype=jnp.int32)[None, :]
    rank = (tile_id - tile_start[tile_e])[:, None] * MOE_TM + row
    valid = (rank < cnt[tile_e][:, None]) & (tile_id < n_tiles)[:, None]
    assign = order[jnp.clip(grp_start[tile_e][:, None] + rank, 0, a - 1)]
    real = (assign & (TOP_K - 1)) * t + lax.shift_right_logical(assign, _K_SHIFT)
    spare = TOP_K * t + (tile_id % 2)[:, None] * MOE_TM + row
    code = jnp.concatenate([TOP_K * t + MOE_TM + row, jnp.where(valid, real, spare)], axis=0)
    return tile_e.astype(jnp.int32), n_tiles.reshape(1).astype(jnp.int32), code.reshape(-1).astype(jnp.int32)


def _ffn_kernel(te_ref, nt_ref, code_ref, h2_hbm, w1_ref, b1_ref, w2_ref, b2_ref, yk_hbm,
                xb0, xb1, yb0, yb1, w1b, w2b, gsem, ssem, *, t):
    i = pl.program_id(0)
    nts = pl.num_programs(0)
    nt = nt_ref[0]
    xbuf = (xb0, xb1)
    ybuf = (yb0, yb1)
    rc = _ROW_CHUNKS
    spare0 = TOP_K * t * rc

    def tile_rows(ref, row):
        return ref.at[pl.ds(pl.multiple_of(row * rc, rc), rc)]

    def start_gather(tile, r, s, priority=0):
        tok = code_ref[(tile + 1) * MOE_TM + r] & (t - 1)
        pltpu.make_async_copy(tile_rows(h2_hbm, tok), tile_rows(xbuf[s], r), gsem.at[s]).start(priority=priority)

    def start_scatter(tile, r, s, priority=0):
        dst = code_ref[(tile + 1) * MOE_TM + r]
        pltpu.make_async_copy(tile_rows(ybuf[s], r), tile_rows(yk_hbm, dst), ssem.at[s]).start(priority=priority)

    def wait_gather(s):
        pltpu.make_async_copy(h2_hbm.at[pl.ds(0, MOE_TM * rc)], xbuf[s], gsem.at[s]).wait()

    def wait_scatter(s):
        pltpu.make_async_copy(ybuf[s], yk_hbm.at[pl.ds(0, MOE_TM * rc)], ssem.at[s]).wait()

    def loop_rows(fn):
        def body(r, carry):
            fn(r)
            return carry

        lax.fori_loop(0, MOE_TM, body, 0, unroll=8)

    @pl.when(i == 0)
    def _():
        yb0[...] = jnp.zeros_like(yb0)
        yb1[...] = jnp.zeros_like(yb1)
        pltpu.make_async_copy(yb0, yk_hbm.at[pl.ds(spare0, MOE_TM * rc)], ssem.at[0]).start()
        loop_rows(lambda r: start_gather(0, r, 0))

    def step(slot):
        other = 1 - slot
        x = jnp.concatenate([xbuf[slot][pl.ds(c, MOE_TM, stride=rc), :].astype(_BF) for c in range(rc)], axis=-1)
        nxt = jnp.minimum(i + 1, nts - 1)
        for r in range(MOE_TM):
            start_gather(nxt, r, other, priority=r % 2)
            start_scatter(i - 1, r, other, priority=(r + 1) % 2)
        gu = jnp.dot(x, w1b[...], preferred_element_type=_F32) + b1_ref[...]
        gt = jnp.minimum(gu[:, :D_FF], SWIGLU_LIMIT)
        up = jnp.clip(gu[:, D_FF:], -SWIGLU_LIMIT, SWIGLU_LIMIT)
        act = gt * jax.nn.sigmoid(SWIGLU_ALPHA * gt) * (up + 1.0)
        y = jnp.dot(act.astype(_BF), w2b[...], preferred_element_type=_F32) + b2_ref[...]
        wait_scatter(slot)
        for c in range(rc):
            ybuf[slot][pl.ds(c, MOE_TM, stride=rc), :] = y[:, c * LANES:(c + 1) * LANES]

    @pl.when(i < nt)
    def _():
        e_changed = jnp.logical_or(i == 0, te_ref[i] != te_ref[jnp.maximum(i - 1, 0)])

        @pl.when(e_changed)
        def _():
            w1b[...] = w1_ref[...].astype(_BF)
            w2b[...] = w2_ref[...].astype(_BF)

        for slot in range(2):
            @pl.when(i % 2 == slot)
            def _(slot=slot):
                wait_gather(slot)
                step(slot)

    @pl.when(i == nts - 1)
    def _():
        last = nt - 1
        for slot in range(2):
            @pl.when(last % 2 == slot)
            def _(slot=slot):
                loop_rows(lambda r: start_scatter(last, r, slot))
                wait_gather(1 - slot)

        wait_scatter(0)
        wait_scatter(1)


def _ffn(h2, tile_e, n_tiles, code, moe_w1, moe_w2, wts, layer):
    t = h2.shape[0] // _ROW_CHUNKS
    nts = tile_e.shape[0]
    by_expert = lambda i, te, nt, code: (te[i], 0, 0)
    by_layer_expert = lambda i, te, nt, code: (layer, te[i], 0, 0)
    grid_spec = pltpu.PrefetchScalarGridSpec(
        num_scalar_prefetch=3,
        grid=(nts,),
        in_specs=[
            pl.BlockSpec(memory_space=pl.ANY),
            pl.BlockSpec((None, None, D_MODEL, 2 * D_FF), by_layer_expert),
            pl.BlockSpec((None, 1, 2 * D_FF), by_expert),
            pl.BlockSpec((None, None, D_FF, D_MODEL), by_layer_expert),
            pl.BlockSpec((None, 1, D_MODEL), by_expert),
        ],
        out_specs=pl.BlockSpec(memory_space=pl.ANY),
        scratch_shapes=[
            pltpu.VMEM((MOE_TM * _ROW_CHUNKS, LANES), _F32),
            pltpu.VMEM((MOE_TM * _ROW_CHUNKS, LANES), _F32),
            pltpu.VMEM((MOE_TM * _ROW_CHUNKS, LANES), _F32),
            pltpu.VMEM((MOE_TM * _ROW_CHUNKS, LANES), _F32),
            pltpu.VMEM((D_MODEL, 2 * D_FF), _BF),
            pltpu.VMEM((D_FF, D_MODEL), _BF),
            pltpu.SemaphoreType.DMA((2,)),
            pltpu.SemaphoreType.DMA((2,)),
        ],
    )
    return pl.pallas_call(
        functools.partial(_ffn_kernel, t=t),
        grid_spec=grid_spec,
        out_shape=jax.ShapeDtypeStruct(((TOP_K * t + 2 * MOE_TM) * _ROW_CHUNKS, LANES), _F32),
        compiler_params=_cparams("arbitrary"),
        name="moe_ffn",
    )(tile_e, n_tiles, code, h2, moe_w1, wts["moe_b1"], moe_w2, wts["moe_b2"])


def _combine_kernel(x1c_ref, x1l_ref, tw_ref, mod_ref, y0_ref, y1_ref, y2_ref, y3_ref, oc_ref, ol_ref,
                    *, tm, ctx_tiles):
    y_refs = (y0_ref, y1_ref, y2_ref, y3_ref)

    def run(x1_ref, o_ref):
        tw = tw_ref[...]
        wk = [jnp.broadcast_to(tw[:, k:k + 1], (tm, LANES)) for k in range(TOP_K)]
        for c in range(_ROW_CHUNKS):
            cols = slice(c * LANES, (c + 1) * LANES)
            acc = wk[0] * y_refs[0][pl.ds(c, tm, stride=_ROW_CHUNKS), :]
            for k in range(1, TOP_K):
                acc = acc + wk[k] * y_refs[k][pl.ds(c, tm, stride=_ROW_CHUNKS), :]
            o_ref[:, cols] = x1_ref[:, cols] + mod_ref[5:6, cols] * acc

    i = pl.program_id(0)

    @pl.when(i < ctx_tiles)
    def _():
        run(x1c_ref, oc_ref)

    @pl.when(i >= ctx_tiles)
    def _():
        run(x1l_ref, ol_ref)


def _combine(x1c, x1l, tw, mods_l, yk, *, tm, seq):
    ctx_tiles = x1c.shape[0] // tm
    lat_tiles = x1l.shape[0] // tm
    nt = ctx_tiles + lat_tiles
    per_seq = seq // tm
    ctx = lambda i: (jnp.minimum(i, ctx_tiles - 1), 0)
    lat = lambda i: (jnp.maximum(i - ctx_tiles, 0), 0)
    mod_idx = lambda i: (jnp.where(i < ctx_tiles, 0, 1 + (i - ctx_tiles) // per_seq), 0, 0)
    plane = lambda k: (lambda i: (k * nt + i, 0))
    return pl.pallas_call(
        functools.partial(_combine_kernel, tm=tm, ctx_tiles=ctx_tiles),
        grid=(nt,),
        in_specs=[pl.BlockSpec((tm, D_MODEL), ctx), pl.BlockSpec((tm, D_MODEL), lat),
                  pl.BlockSpec((tm, LANES), lambda i: (i, 0)),
                  pl.BlockSpec((None, 6, D_MODEL), mod_idx)]
                 + [pl.BlockSpec((tm * _ROW_CHUNKS, LANES), plane(k)) for k in range(TOP_K)],
        out_specs=[pl.BlockSpec((tm, D_MODEL), ctx), pl.BlockSpec((tm, D_MODEL), lat)],
        out_shape=[jax.ShapeDtypeStruct(x1c.shape, _F32), jax.ShapeDtypeStruct(x1l.shape, _F32)],
        compiler_params=_cparams("arbitrary"),
        name="moe_combine",
    )(x1c, x1l, tw, mods_l, yk, yk, yk, yk)


def _pad_heads(w, heads, dim):
    lead = w.shape[:-1]
    w = w.reshape(lead + (heads, dim))
    w = jnp.pad(w, [(0, 0)] * len(lead) + [(0, 0), (0, LANES - dim)])
    return w.reshape(lead + (heads * LANES,))


def _pad_lanes(g, left=0):
    return jnp.pad(g, (left, LANES - left - g.shape[0])).reshape(1, LANES)


def _block_diag_halves(w):
    w4 = w.reshape(2, 4, 64, 64)
    eye = jnp.eye(4, dtype=w.dtype)
    return jnp.einsum("jaik,ab->jaibk", w4, eye).reshape(2, 256, 256)


def _layer_weights(l, p):
    w_in = p["w_in"][l]
    qa, ka, va, xb, yb, cq, ckv, kr = jnp.split(w_in, (512, 640, 768, 1280, 1792, 2176, 2432), axis=-1)
    kr128 = jnp.pad(kr, ((0, 0), (MLA_NOPE, LANES - MLA_NOPE - MLA_ROPE)))
    w_in_p = jnp.concatenate([_pad_heads(qa, A_HEADS, A_HEAD_DIM), _pad_heads(ka, A_KV_HEADS, A_HEAD_DIM), va, xb,
                              yb, cq, ckv, kr128], axis=-1).astype(_BF)
    ukv = p["mla_w_ukv"][l].reshape(KV_LORA, MLA_HEADS, MLA_NOPE + MLA_V)
    w_ukv = jnp.concatenate([_pad_heads(ukv[:, :, :MLA_NOPE].reshape(KV_LORA, -1), MLA_HEADS, MLA_NOPE),
                             ukv[:, :, MLA_NOPE:].reshape(KV_LORA, -1)], axis=-1).astype(_BF)
    wg = jnp.concatenate([_block_diag_halves(p["lru_wa"][l, 0]), _block_diag_halves(p["lru_wx"][l, 0]),
                          _block_diag_halves(p["lru_wa"][l, 1]), _block_diag_halves(p["lru_wx"][l, 1])],
                         axis=-1).astype(_BF)
    bg = jnp.stack([p["lru_ba"][l, 0], p["lru_bx"][l, 0], p["lru_ba"][l, 1], p["lru_bx"][l, 1]])
    return {
        "norm1_g": p["norm1_g"][l].reshape(1, -1),
        "norm2_g": p["norm2_g"][l].reshape(1, -1),
        "w_in": w_in_p,
        "a_q_g": _pad_lanes(p["a_q_g"][l]),
        "a_k_g": _pad_lanes(p["a_k_g"][l]),
        "mla_cq_g": p["mla_cq_g"][l].reshape(1, -1),
        "w_uq": _pad_heads(p["mla_w_uq"][l], MLA_HEADS, MLA_QK).astype(_BF),
        "mla_q_g": _pad_lanes(p["mla_q_g"][l]),
        "mla_ckv_g": p["mla_ckv_g"][l].reshape(1, -1),
        "w_ukv": w_ukv,
        "mla_k_g": _pad_lanes(p["mla_k_g"][l]),
        "conv_w": p["conv_w"][l],
        "conv_b": p["conv_b"][l].reshape(1, -1),
        "lru_wg": wg,
        "lru_bg": bg,
        "lru_lam": p["lru_lam"][l],
        "w_out": p["w_out"][l].astype(_BF),
        "router_w": jnp.pad(p["router_w"][l], ((0, 0), (0, LANES - N_EXPERTS))),
        "router_b": jnp.pad(p["router_b"][l], (0, LANES - N_EXPERTS), constant_values=-1e30).reshape(1, LANES),
        "moe_b1": p["moe_b1"][l].reshape(N_EXPERTS, 1, 2 * D_FF),
        "moe_b2": p["moe_b2"][l].reshape(N_EXPERTS, 1, D_MODEL),
    }


def _rope_tables(n, rot_dim, lane0):
    rows = n // GRID_W
    r = jnp.repeat(jnp.arange(rows, dtype=_F32), GRID_W)
    col = jnp.tile(jnp.arange(GRID_W, dtype=_F32), rows)
    n_freq = rot_dim // 4
    inv = ROPE_THETA ** (-jnp.arange(n_freq, dtype=_F32) / n_freq)
    ang = jnp.concatenate([r[:, None] * inv, col[:, None] * inv], axis=-1)
    cos = jnp.repeat(jnp.cos(ang), 2, axis=-1)
    sin = jnp.repeat(jnp.sin(ang), 2, axis=-1) * jnp.tile(jnp.array([-1.0, 1.0], _F32), rot_dim // 2)
    pad = ((0, 0), (lane0, LANES - lane0 - rot_dim))
    return jnp.pad(cos, pad, constant_values=1.0), jnp.pad(sin, pad)


def kernel(x_prompt, x_sample, cache_attn_k, cache_attn_v, state_lru, cache_mla_ckv, cache_mla_krope, c, c_ctx,
           ada_w, ada_b, norm1_g, norm2_g, w_in, a_q_g, a_k_g, conv_w, conv_b, lru_wa, lru_ba, lru_wx, lru_bx,
           lru_lam, mla_cq_g, mla_w_uq, mla_q_g, mla_ckv_g, mla_w_ukv, mla_k_g, w_out, router_w, router_b,
           moe_w1, moe_b1, moe_w2, moe_b2):
    p = dict(norm1_g=norm1_g, norm2_g=norm2_g, w_in=w_in, a_q_g=a_q_g, a_k_g=a_k_g, conv_w=conv_w, conv_b=conv_b,
             lru_wa=lru_wa, lru_ba=lru_ba, lru_wx=lru_wx, lru_bx=lru_bx, lru_lam=lru_lam, mla_cq_g=mla_cq_g,
             mla_w_uq=mla_w_uq, mla_q_g=mla_q_g, mla_ckv_g=mla_ckv_g, mla_w_ukv=mla_w_ukv, mla_k_g=mla_k_g,
             w_out=w_out, router_w=router_w, router_b=router_b, moe_w1=moe_w1, moe_b1=moe_b1, moe_w2=moe_w2,
             moe_b2=moe_b2)
    batch, seq, _ = x_prompt.shape
    dec_batch, dec_seq, _ = x_sample.shape
    past = cache_attn_k.shape[2]

    cvec = jnp.concatenate([c_ctx[None, :], c, jnp.zeros((N_MODS - 1 - dec_batch, D_MODEL), _F32)], axis=0)
    mods = _ada_mods(cvec, ada_w, ada_b)
    cos_a, sin_a = _rope_tables(dec_seq, A_HEAD_DIM, 0)
    cos_c, sin_c = _rope_tables(dec_seq, MLA_ROPE, MLA_NOPE)
    tabs = (cos_a, sin_a, cos_c, sin_c)

    yp = x_prompt.reshape(batch * seq, D_MODEL)
    ys = x_sample.reshape(dec_batch * dec_seq, D_MODEL)
    zero_state = jnp.zeros((batch, 2, LRU_WIDTH), _F32)
    ks_l, vs_l, hs_l, ckv_l, kr_l = [], [], [], [], []
    tm = 512
    tm_combine = 256
    tq = 512
    n_tok = batch * seq + dec_batch * dec_seq
    for l in range(DEPTH):
        wts = _layer_weights(l, p)
        mods_l = mods[l]

        qa, kab, kaf, vab, vaf, xb, yb, qc, ckvn, kr128, kr = _proj(yp, mods_l, wts, None, rope=False, tm=tm, seq=seq)
        kc, vc = _mlakv(ckvn, kr128, wts, None, rope=False, tm=tm, seq=seq)
        ob, st = _lru(xb, yb, zero_state, wts, n=seq)
        oa = _attn(qa, kab, vab, None, None, heads=A_HEADS, kv_heads=A_KV_HEADS, dv=A_HEAD_DIM, n=seq, tq=seq,
                   name="attn_a_ctx")
        oc = _attn(qc, kc, vc, None, None, heads=MLA_HEADS, kv_heads=MLA_HEADS, dv=MLA_V, n=seq, tq=seq,
                   name="attn_c_ctx")
        ctx_ops = (oa, ob, oc, yp)
        ks_l.append(kaf.reshape(batch, seq, A_KV_HEADS, A_HEAD_DIM))
        vs_l.append(vaf.reshape(batch, seq, A_KV_HEADS, A_HEAD_DIM))
        hs_l.append(st)
        ckv_l.append(ckvn.reshape(batch, seq, KV_LORA))
        kr_l.append(kr.reshape(batch, seq, MLA_ROPE))

        qa, kab, _, vab, _, xb, yb, qc, ckvn, kr128, _ = _proj(ys, mods_l, wts, tabs, rope=True, tm=tm, seq=dec_seq)
        kc, vc = _mlakv(ckvn, kr128, wts, tabs, rope=True, tm=tm, seq=dec_seq)
        c_kr128 = jnp.pad(cache_mla_krope[:, l].reshape(dec_batch * past, MLA_ROPE),
                          ((0, 0), (MLA_NOPE, LANES - MLA_NOPE - MLA_ROPE)))
        kx, vx = _mlakv(cache_mla_ckv[:, l].reshape(dec_batch * past, KV_LORA), c_kr128, wts, None, rope=False,
                        tm=tm, seq=past)
        ob, _ = _lru(xb, yb, state_lru[:, l], wts, n=dec_seq)
        c_k = _pad_heads(cache_attn_k[:, l].reshape(dec_batch * past, A_KV_HEADS * A_HEAD_DIM), A_KV_HEADS,
                         A_HEAD_DIM).astype(_BF)
        c_v = cache_attn_v[:, l].reshape(dec_batch * past, A_KV_HEADS * A_HEAD_DIM).astype(_BF)
        oa = _attn(qa, kab, vab, c_k, c_v, heads=A_HEADS, kv_heads=A_KV_HEADS, dv=A_HEAD_DIM, n=dec_seq, tq=tq,
                   name="attn_a_lat")
        oc = _attn(qc, kc, vc, kx, vx, heads=MLA_HEADS, kv_heads=MLA_HEADS, dv=MLA_V, n=dec_seq, tq=tq,
                   name="attn_c_lat")
        x1c, x1l, h2, tidx, tw, hist = _post(ctx_ops, (oa, ob, oc, ys), mods_l, wts, tm=tm, seq=dec_seq)
        flat_e = tidx[:, :TOP_K].reshape(n_tok * TOP_K)
        cnt = jnp.sum(hist, axis=(0, 1))[:N_EXPERTS].astype(jnp.int32)
        tile_e, n_tiles, code = _route(flat_e, cnt)
        yk = _ffn(h2, tile_e, n_tiles, code, moe_w1, moe_w2, wts, l)
        yp, ys = _combine(x1c, x1l, tw, mods_l, yk, tm=tm_combine, seq=dec_seq)

    return (yp.reshape(batch, seq, D_MODEL), ys.reshape(dec_batch, dec_seq, D_MODEL),
            jnp.stack(ks_l, axis=1), jnp.stack(vs_l, axis=1), jnp.stack(hs_l, axis=1),
            jnp.stack(ckv_l, axis=1), jnp.stack(kr_l, axis=1))
```

```python
import functools

import jax
import jax.numpy as jnp
from jax import lax
from jax.experimental import pallas as pl
from jax.experimental.pallas import tpu as pltpu

D_MODEL = 1024
DEPTH = 2
GRID_W = 64
ROPE_THETA = 10000.0
EPS = 1e-6
A_HEADS = 8
A_KV_HEADS = 2
A_HEAD_DIM = 64
LRU_WIDTH = 512
LRU_BLOCKS = 8
CONV_W = 4
LRU_C = 8.0
MLA_HEADS = 8
MLA_NOPE = 64
MLA_ROPE = 32
MLA_QK = MLA_NOPE + MLA_ROPE
MLA_V = 64
Q_LORA = 384
KV_LORA = 256
N_EXPERTS = 32
TOP_K = 4
D_FF = 1024
SWIGLU_LIMIT = 7.0
SWIGLU_ALPHA = 1.702

LANES = 128
N_MODS = 16
VMEM_LIMIT = 56 * 1024 * 1024

_QA0 = 0
_KA0 = _QA0 + A_HEADS * LANES
_VA0 = _KA0 + A_KV_HEADS * LANES
_XB0 = _VA0 + A_KV_HEADS * A_HEAD_DIM
_YB0 = _XB0 + LRU_WIDTH
_CQ0 = _YB0 + LRU_WIDTH
_CKV0 = _CQ0 + Q_LORA
_KR0 = _CKV0 + KV_LORA
_NP = _KR0 + LANES

_BF = jnp.bfloat16
_F32 = jnp.float32


def _cparams(*sem):
    return pltpu.CompilerParams(dimension_semantics=sem, vmem_limit_bytes=VMEM_LIMIT)


def _split_dot(a, b):
    a_hi = a.astype(_BF)
    a_lo = (a - a_hi.astype(_F32)).astype(_BF)
    b_hi = b.astype(_BF)
    b_lo = (b - b_hi.astype(_F32)).astype(_BF)
    d = functools.partial(jnp.dot, preferred_element_type=_F32)
    return d(a_hi, b_hi) + (d(a_hi, b_lo) + d(a_lo, b_hi))


def _rope_rotate(x, cos, sin_signed):
    lane = lax.broadcasted_iota(jnp.int32, x.shape, 1)
    swapped = jnp.where((lane & 1) == 0, pltpu.roll(x, LANES - 1, 1), pltpu.roll(x, 1, 1))
    return x * cos + swapped * sin_signed


def _head_rms(xh, denom):
    return xh * lax.rsqrt(jnp.sum(xh * xh, axis=-1, keepdims=True) * (1.0 / denom) + EPS)


def _mods_kernel(c_ref, w_ref, b_ref, o_ref):
    c = c_ref[...]
    s = c * jax.nn.sigmoid(c)
    o_ref[...] = _split_dot(s, w_ref[...]) + b_ref[...]


def _ada_mods(cvec, ada_w, ada_b):
    tn = 1536
    n_out = ada_w.shape[-1]
    out = pl.pallas_call(
        _mods_kernel,
        grid=(DEPTH, n_out // tn),
        in_specs=[
            pl.BlockSpec((N_MODS, D_MODEL), lambda l, j: (0, 0)),
            pl.BlockSpec((None, D_MODEL, tn), lambda l, j: (l, 0, j)),
            pl.BlockSpec((None, 1, tn), lambda l, j: (l, 0, j)),
        ],
        out_specs=pl.BlockSpec((None, N_MODS, tn), lambda l, j: (l, 0, j)),
        out_shape=jax.ShapeDtypeStruct((DEPTH, N_MODS, n_out), _F32),
        compiler_params=_cparams("arbitrary", "arbitrary"),
        name="ada_mods",
    )(cvec, ada_w, ada_b.reshape(DEPTH, 1, n_out))
    return out.reshape(DEPTH, N_MODS, 6, D_MODEL)


def _proj_kernel(*refs, rope):
    if rope:
        (x_ref, mod_ref, n1g_ref, win_ref, qg_ref, kg_ref, cqg_ref, wuq_ref, mqg_ref, ckvg_ref,
         cosa_ref, sina_ref, cosc_ref, sinc_ref,
         qa_ref, kab_ref, kaf_ref, vab_ref, vaf_ref, xb_ref, yb_ref, qc_ref, ckvn_ref, kr128_ref, kr_ref) = refs
    else:
        (x_ref, mod_ref, n1g_ref, win_ref, qg_ref, kg_ref, cqg_ref, wuq_ref, mqg_ref, ckvg_ref,
         qa_ref, kab_ref, kaf_ref, vab_ref, vaf_ref, xb_ref, yb_ref, qc_ref, ckvn_ref, kr128_ref, kr_ref) = refs
    x = x_ref[...]
    xn = x * lax.rsqrt(jnp.mean(x * x, axis=-1, keepdims=True) + EPS) * n1g_ref[...]
    h = xn * (1.0 + mod_ref[1:2, :]) + mod_ref[0:1, :]
    z = jnp.dot(h.astype(_BF), win_ref[...], preferred_element_type=_F32)

    for hd in range(A_HEADS):
        q = _head_rms(z[:, _QA0 + hd * LANES:_QA0 + (hd + 1) * LANES], A_HEAD_DIM) * qg_ref[...]
        if rope:
            q = _rope_rotate(q, cosa_ref[...], sina_ref[...])
        qa_ref[:, hd * LANES:(hd + 1) * LANES] = (q * (A_HEAD_DIM ** -0.5)).astype(_BF)
    k_heads = []
    for hd in range(A_KV_HEADS):
        k = _head_rms(z[:, _KA0 + hd * LANES:_KA0 + (hd + 1) * LANES], A_HEAD_DIM) * kg_ref[...]
        k_heads.append(k[:, :A_HEAD_DIM])
        if rope:
            k = _rope_rotate(k, cosa_ref[...], sina_ref[...])
        kab_ref[:, hd * LANES:(hd + 1) * LANES] = k.astype(_BF)
    kaf_ref[...] = jnp.concatenate(k_heads, axis=-1)
    va = z[:, _VA0:_VA0 + A_KV_HEADS * A_HEAD_DIM]
    vaf_ref[...] = va
    vab_ref[...] = va.astype(_BF)

    xb_ref[...] = z[:, _XB0:_XB0 + LRU_WIDTH]
    yb_ref[...] = z[:, _YB0:_YB0 + LRU_WIDTH]

    cq = z[:, _CQ0:_CQ0 + Q_LORA]
    cqn = cq * lax.rsqrt(jnp.mean(cq * cq, axis=-1, keepdims=True) + EPS) * cqg_ref[...]
    zq = jnp.dot(cqn.astype(_BF), wuq_ref[...], preferred_element_type=_F32)
    for hd in range(MLA_HEADS):
        q = _head_rms(zq[:, hd * LANES:(hd + 1) * LANES], MLA_QK) * mqg_ref[...]
        if rope:
            q = _rope_rotate(q, cosc_ref[...], sinc_ref[...])
        qc_ref[:, hd * LANES:(hd + 1) * LANES] = (q * (MLA_QK ** -0.5)).astype(_BF)
    ckv = z[:, _CKV0:_CKV0 + KV_LORA]
    ckvn_ref[...] = ckv * lax.rsqrt(jnp.mean(ckv * ckv, axis=-1, keepdims=True) + EPS) * ckvg_ref[...]
    kr128 = z[:, _KR0:_KR0 + LANES]
    kr128_ref[...] = kr128
    kr_ref[...] = kr128[:, MLA_NOPE:MLA_NOPE + MLA_ROPE]


def _proj(x, mods_l, wts, tabs, *, rope, tm, seq):
    t = x.shape[0]
    nt = t // tm
    per_seq = seq // tm

    def row(i):
        return (i, 0)

    def const2(i):
        return (0, 0)

    if rope:
        def mod_idx(i):
            return (1 + i // per_seq, 0, 0)

        def tab_idx(i):
            return (i % per_seq, 0)
    else:
        def mod_idx(i):
            return (0, 0, 0)

    in_specs = [
        pl.BlockSpec((tm, D_MODEL), row),
        pl.BlockSpec((None, 6, D_MODEL), mod_idx),
        pl.BlockSpec((1, D_MODEL), const2),
        pl.BlockSpec((D_MODEL, _NP), const2),
        pl.BlockSpec((1, LANES), const2),
        pl.BlockSpec((1, LANES), const2),
        pl.BlockSpec((1, Q_LORA), const2),
        pl.BlockSpec((Q_LORA, MLA_HEADS * LANES), const2),
        pl.BlockSpec((1, LANES), const2),
        pl.BlockSpec((1, KV_LORA), const2),
    ]
    args = [x, mods_l, wts["norm1_g"], wts["w_in"], wts["a_q_g"], wts["a_k_g"], wts["mla_cq_g"], wts["w_uq"],
            wts["mla_q_g"], wts["mla_ckv_g"]]
    if rope:
        in_specs += [pl.BlockSpec((tm, LANES), tab_idx)] * 4
        args += list(tabs)
    widths = [(A_HEADS * LANES, _BF), (A_KV_HEADS * LANES, _BF), (A_KV_HEADS * A_HEAD_DIM, _F32),
              (A_KV_HEADS * A_HEAD_DIM, _BF), (A_KV_HEADS * A_HEAD_DIM, _F32), (LRU_WIDTH, _F32), (LRU_WIDTH, _F32),
              (MLA_HEADS * LANES, _BF), (KV_LORA, _F32), (LANES, _F32), (MLA_ROPE, _F32)]
    out_specs = [pl.BlockSpec((tm, w), row) for w, _ in widths]
    out_shape = [jax.ShapeDtypeStruct((t, w), dt) for w, dt in widths]
    return pl.pallas_call(
        functools.partial(_proj_kernel, rope=rope),
        grid=(nt,),
        in_specs=in_specs,
        out_specs=out_specs,
        out_shape=out_shape,
        compiler_params=_cparams("arbitrary"),
        name="proj_rope" if rope else "proj",
    )(*args)


def _mlakv_kernel(*refs, rope):
    if rope:
        ckvn_ref, kr128_ref, wukv_ref, kg_ref, cos_ref, sin_ref, k_ref, v_ref = refs
    else:
        ckvn_ref, kr128_ref, wukv_ref, kg_ref, k_ref, v_ref = refs
    kv = jnp.dot(ckvn_ref[...].astype(_BF), wukv_ref[...], preferred_element_type=_F32)
    kr128 = kr128_ref[...]
    for hd in range(MLA_HEADS):
        k = _head_rms(kv[:, hd * LANES:(hd + 1) * LANES] + kr128, MLA_QK) * kg_ref[...]
        if rope:
            k = _rope_rotate(k, cos_ref[...], sin_ref[...])
        k_ref[:, hd * LANES:(hd + 1) * LANES] = k.astype(_BF)
    v_ref[...] = kv[:, MLA_HEADS * LANES:].astype(_BF)


def _mlakv(ckvn, kr128, wts, tabs, *, rope, tm, seq):
    t = ckvn.shape[0]
    per_seq = seq // tm
    in_specs = [
        pl.BlockSpec((tm, KV_LORA), lambda i: (i, 0)),
        pl.BlockSpec((tm, LANES), lambda i: (i, 0)),
        pl.BlockSpec((KV_LORA, MLA_HEADS * (LANES + MLA_V)), lambda i: (0, 0)),
        pl.BlockSpec((1, LANES), lambda i: (0, 0)),
    ]
    args = [ckvn, kr128, wts["w_ukv"], wts["mla_k_g"]]
    if rope:
        in_specs += [pl.BlockSpec((tm, LANES), lambda i: (i % per_seq, 0))] * 2
        args += [tabs[2], tabs[3]]
    return pl.pallas_call(
        functools.partial(_mlakv_kernel, rope=rope),
        grid=(t // tm,),
        in_specs=in_specs,
        out_specs=[pl.BlockSpec((tm, MLA_HEADS * LANES), lambda i: (i, 0)),
                   pl.BlockSpec((tm, MLA_HEADS * MLA_V), lambda i: (i, 0))],
        out_shape=[jax.ShapeDtypeStruct((t, MLA_HEADS * LANES), _BF),
                   jax.ShapeDtypeStruct((t, MLA_HEADS * MLA_V), _BF)],
        compiler_params=_cparams("arbitrary"),
        name="mlakv_rope" if rope else "mlakv",
    )(*args)


_CHUNK = 8


def _lru_kernel(xb_ref, yb_ref, h0_ref, cw_ref, cb_ref, wg_ref, bg_ref, lam_ref,
                ob_ref, st_ref, af_ref, bf_ref, ab_ref, bb_ref, h_ref, *, n):
    x = xb_ref[...]
    rows = lax.broadcasted_iota(jnp.int32, x.shape, 0)
    xc = cb_ref[...] + x * cw_ref[2:3, :]
    xc = xc + jnp.where(rows >= 2, pltpu.roll(x, 2, 0), 0.0) * cw_ref[0:1, :]
    xc = xc + jnp.where(rows >= 1, pltpu.roll(x, 1, 0), 0.0) * cw_ref[1:2, :]
    xc = xc + jnp.where(rows < n - 1, pltpu.roll(x, n - 1, 0), 0.0) * cw_ref[3:4, :]

    xcb = xc.astype(_BF)
    half = LRU_WIDTH // 2
    pre = [jnp.dot(xcb[:, j * half:(j + 1) * half], wg_ref[j], preferred_element_type=_F32) for j in range(2)]
    for d, (a_ref, b_ref) in enumerate(((af_ref, bf_ref), (ab_ref, bb_ref))):
        pa = jnp.concatenate([pre[0][:, (2 * d) * half:(2 * d + 1) * half],
                              pre[1][:, (2 * d) * half:(2 * d + 1) * half]], axis=-1)
        px = jnp.concatenate([pre[0][:, (2 * d + 1) * half:(2 * d + 2) * half],
                              pre[1][:, (2 * d + 1) * half:(2 * d + 2) * half]], axis=-1)
        r = jax.nn.sigmoid(pa + bg_ref[2 * d:2 * d + 1, :])
        i = jax.nn.sigmoid(px + bg_ref[2 * d + 1:2 * d + 2, :])
        nlam = -lam_ref[d:d + 1, :]
        softplus = jnp.maximum(nlam, 0.0) + jnp.log1p(jnp.exp(-jnp.abs(nlam)))
        log_a = ((-LRU_C) * softplus) * r
        a = jnp.exp(log_a)
        a_ref[...] = a
        gain2 = -jnp.tanh(log_a) * (a * a + 1.0)
        b_ref[...] = jnp.where(gain2 > 0.0, gain2 * lax.rsqrt(gain2), 0.0) * i * xc

    nchunks = n // _CHUNK
    crow = lax.broadcasted_iota(jnp.int32, (_CHUNK, LRU_WIDTH), 0)

    def chunk_scan(a, b, carry, reverse):
        for d in (1, 2, 4):
            if reverse:
                keep = crow < _CHUNK - d
                a_s = pltpu.roll(a, _CHUNK - d, 0)
                b_s = pltpu.roll(b, _CHUNK - d, 0)
            else:
                keep = crow >= d
                a_s = pltpu.roll(a, d, 0)
                b_s = pltpu.roll(b, d, 0)
            b = jnp.where(keep, a * b_s + b, b)
            a = jnp.where(keep, a * a_s, a)
        return a * carry + b

    def body(ci, carry):
        cf, cb = carry
        off_f = pl.multiple_of(ci * _CHUNK, _CHUNK)
        off_b = pl.multiple_of((nchunks - 1 - ci) * _CHUNK, _CHUNK)
        hf = chunk_scan(af_ref[pl.ds(off_f, _CHUNK), :], bf_ref[pl.ds(off_f, _CHUNK), :], cf, False)
        hb = chunk_scan(ab_ref[pl.ds(off_b, _CHUNK), :], bb_ref[pl.ds(off_b, _CHUNK), :], cb, True)
        bf_ref[pl.ds(off_f, _CHUNK), :] = hf
        bb_ref[pl.ds(off_b, _CHUNK), :] = hb
        return hf[_CHUNK - 1:_CHUNK, :], hb[0:1, :]

    cf, cb = lax.fori_loop(0, nchunks, body, (h0_ref[0:1, :], h0_ref[1:2, :]))
    st_ref[0:1, :] = cf
    st_ref[1:2, :] = cb
    ob_ref[...] = ((bf_ref[...] + bb_ref[...]) * jax.nn.gelu(yb_ref[...])).astype(_BF)


def _lru(xb, yb, h0, wts, *, n):
    nb = xb.shape[0] // n
    seq = lambda b: (b, 0)
    const2 = lambda b: (0, 0)
    return pl.pallas_call(
        functools.partial(_lru_kernel, n=n),
        grid=(nb,),
        in_specs=[
            pl.BlockSpec((n, LRU_WIDTH), seq),
            pl.BlockSpec((n, LRU_WIDTH), seq),
            pl.BlockSpec((None, 2, LRU_WIDTH), lambda b: (b, 0, 0)),
            pl.BlockSpec((CONV_W, LRU_WIDTH), const2),
            pl.BlockSpec((1, LRU_WIDTH), const2),
            pl.BlockSpec((2, LRU_WIDTH // 2, 2 * LRU_WIDTH), lambda b: (0, 0, 0)),
            pl.BlockSpec((4, LRU_WIDTH), const2),
            pl.BlockSpec((2, LRU_WIDTH), const2),
        ],
        out_specs=[pl.BlockSpec((n, LRU_WIDTH), seq),
                   pl.BlockSpec((None, 2, LRU_WIDTH), lambda b: (b, 0, 0))],
        out_shape=[jax.ShapeDtypeStruct((nb * n, LRU_WIDTH), _BF),
                   jax.ShapeDtypeStruct((nb, 2, LRU_WIDTH), _F32)],
        scratch_shapes=[pltpu.VMEM((n, LRU_WIDTH), _F32) for _ in range(5)],
        compiler_params=_cparams("arbitrary"),
        name=f"lru_{n}",
    )(xb, yb, h0, wts["conv_w"], wts["conv_b"], wts["lru_wg"], wts["lru_bg"], wts["lru_lam"])


def _attn_kernel(*refs, heads, kv_heads, dv, cached):
    if cached:
        q_ref, k_ref, v_ref, kc_ref, vc_ref, o_ref = refs
    else:
        q_ref, k_ref, v_ref, o_ref = refs
    group = heads // kv_heads
    contract_last = (((1,), (1,)), ((), ()))
    outs = []
    for hd in range(heads):
        g = hd // group
        q = q_ref[:, hd * LANES:(hd + 1) * LANES]
        s = lax.dot_general(q, k_ref[:, g * LANES:(g + 1) * LANES], contract_last, preferred_element_type=_F32)
        m = jnp.max(s, axis=-1, keepdims=True)
        if cached:
            sc = lax.dot_general(q, kc_ref[:, g * LANES:(g + 1) * LANES], contract_last,
                                 preferred_element_type=_F32)
            m = jnp.maximum(m, jnp.max(sc, axis=-1, keepdims=True))
        p = jnp.exp(s - m)
        den = jnp.sum(p, axis=-1, keepdims=True)
        o = jnp.dot(p.astype(_BF), v_ref[:, g * dv:(g + 1) * dv], preferred_element_type=_F32)
        if cached:
            pc = jnp.exp(sc - m)
            den = den + jnp.sum(pc, axis=-1, keepdims=True)
            o = o + jnp.dot(pc.astype(_BF), vc_ref[:, g * dv:(g + 1) * dv], preferred_element_type=_F32)
        outs.append(o / den)
    o_ref[...] = jnp.concatenate(outs, axis=-1).astype(_BF)


def _attn(q, k, v, kc, vc, *, heads, kv_heads, dv, n, tq, name):
    t = q.shape[0]
    nb = t // n
    nq = n // tq
    cached = kc is not None
    in_specs = [
        pl.BlockSpec((tq, heads * LANES), lambda b, i: (b * nq + i, 0)),
        pl.BlockSpec((n, kv_heads * LANES), lambda b, i: (b, 0)),
        pl.BlockSpec((n, kv_heads * dv), lambda b, i: (b, 0)),
    ]
    args = [q, k, v]
    if cached:
        nc = kc.shape[0] // nb
        in_specs += [pl.BlockSpec((nc, kv_heads * LANES), lambda b, i: (b, 0)),
                     pl.BlockSpec((nc, kv_heads * dv), lambda b, i: (b, 0))]
        args += [kc, vc]
    return pl.pallas_call(
        functools.partial(_attn_kernel, heads=heads, kv_heads=kv_heads, dv=dv, cached=cached),
        grid=(nb, nq),
        in_specs=in_specs,
        out_specs=pl.BlockSpec((tq, heads * dv), lambda b, i: (b * nq + i, 0)),
        out_shape=jax.ShapeDtypeStruct((t, heads * dv), _BF),
        compiler_params=_cparams("arbitrary", "arbitrary"),
        name=name,
    )(*args)


def _post_kernel(oac_ref, obc_ref, occ_ref, xc_ref, oal_ref, obl_ref, ocl_ref, xl_ref,
                 mod_ref, wout_ref, n2g_ref, rw_ref, rb_ref,
                 x1c_ref, x1l_ref, h2_ref, tidx_ref, tw_ref, hist_ref, *, tm, ctx_tiles):
    i = pl.program_id(0)

    @pl.when(i < ctx_tiles)
    def _():
        _post_block(oac_ref, obc_ref, occ_ref, xc_ref, mod_ref, wout_ref, n2g_ref, rw_ref, rb_ref,
                    x1c_ref, h2_ref, tidx_ref, tw_ref, hist_ref, tm)

    @pl.when(i >= ctx_tiles)
    def _():
        _post_block(oal_ref, obl_ref, ocl_ref, xl_ref, mod_ref, wout_ref, n2g_ref, rw_ref, rb_ref,
                    x1l_ref, h2_ref, tidx_ref, tw_ref, hist_ref, tm)


def _post_block(oa_ref, ob_ref, oc_ref, x_ref, mod_ref, wout_ref, n2g_ref, rw_ref, rb_ref,
                x1_ref, h2_ref, tidx_ref, tw_ref, hist_ref, tm):
    w0 = A_HEADS * A_HEAD_DIM
    w1 = w0 + LRU_WIDTH
    o = jnp.dot(oa_ref[...], wout_ref[0:w0, :], preferred_element_type=_F32)
    o = o + jnp.dot(ob_ref[...], wout_ref[w0:w1, :], preferred_element_type=_F32)
    o = o + jnp.dot(oc_ref[...], wout_ref[w1:, :], preferred_element_type=_F32)
    x1 = x_ref[...] + mod_ref[2:3, :] * o
    x1_ref[...] = x1
    xn = x1 * lax.rsqrt(jnp.mean(x1 * x1, axis=-1, keepdims=True) + EPS) * n2g_ref[...]
    h2 = xn * (1.0 + mod_ref[4:5, :]) + mod_ref[3:4, :]
    for c in range(_ROW_CHUNKS):
        h2_ref[pl.ds(c, tm, stride=_ROW_CHUNKS), :] = h2[:, c * LANES:(c + 1) * LANES]

    logits = _split_dot(h2, rw_ref[...]) + rb_ref[...]
    lane = lax.broadcasted_iota(jnp.int32, logits.shape, 1).astype(_F32)
    neg = jnp.float32(-jnp.inf)
    work = logits
    top = None
    tidx = jnp.zeros(logits.shape, _F32)
    tw = jnp.zeros(logits.shape, _F32)
    for k in range(TOP_K):
        m = jnp.max(work, axis=-1, keepdims=True)
        if top is None:
            top = m
        first = jnp.min(jnp.where(work == m, lane, float(LANES)), axis=-1, keepdims=True)
        work = jnp.where(lane == first, neg, work)
        tidx = jnp.where(lane == float(k), first, tidx)
        tw = jnp.where(lane == float(k), jnp.exp(m - top), tw)
    tidx_ref[...] = tidx.astype(jnp.int32)
    tw_ref[...] = tw / jnp.sum(tw, axis=-1, keepdims=True)
    hist_ref[...] = jnp.sum(jnp.where(work == neg, 1.0, 0.0), axis=0, keepdims=True)


_ROW_CHUNKS = D_MODEL // LANES


def _post(ctx_ops, lat_ops, mods_l, wts, *, tm, seq):
    ctx_tiles = ctx_ops[3].shape[0] // tm
    lat_tiles = lat_ops[3].shape[0] // tm
    nt = ctx_tiles + lat_tiles
    t = nt * tm
    per_seq = seq // tm
    row = lambda i: (i, 0)
    ctx = lambda i: (jnp.minimum(i, ctx_tiles - 1), 0)
    lat = lambda i: (jnp.maximum(i - ctx_tiles, 0), 0)
    const2 = lambda i: (0, 0)
    mod_idx = lambda i: (jnp.where(i < ctx_tiles, 0, 1 + (i - ctx_tiles) // per_seq), 0, 0)
    mix = A_HEADS * A_HEAD_DIM + LRU_WIDTH + MLA_HEADS * MLA_V
    widths = (A_HEADS * A_HEAD_DIM, LRU_WIDTH, MLA_HEADS * MLA_V, D_MODEL)
    in_specs = ([pl.BlockSpec((tm, w), ctx) for w in widths] + [pl.BlockSpec((tm, w), lat) for w in widths] + [
        pl.BlockSpec((None, 6, D_MODEL), mod_idx),
        pl.BlockSpec((mix, D_MODEL), const2),
        pl.BlockSpec((1, D_MODEL), const2),
        pl.BlockSpec((D_MODEL, LANES), const2),
        pl.BlockSpec((1, LANES), const2),
    ])
    return pl.pallas_call(
        functools.partial(_post_kernel, tm=tm, ctx_tiles=ctx_tiles),
        grid=(nt,),
        in_specs=in_specs,
        out_specs=[pl.BlockSpec((tm, D_MODEL), ctx), pl.BlockSpec((tm, D_MODEL), lat),
                   pl.BlockSpec((tm * _ROW_CHUNKS, LANES), row),
                   pl.BlockSpec((tm, LANES), row), pl.BlockSpec((tm, LANES), row),
                   pl.BlockSpec((None, 1, LANES), lambda i: (i, 0, 0))],
        out_shape=[jax.ShapeDtypeStruct((ctx_tiles * tm, D_MODEL), _F32),
                   jax.ShapeDtypeStruct((lat_tiles * tm, D_MODEL), _F32),
                   jax.ShapeDtypeStruct((t * _ROW_CHUNKS, LANES), _F32),
                   jax.ShapeDtypeStruct((t, LANES), jnp.int32), jax.ShapeDtypeStruct((t, LANES), _F32),
                   jax.ShapeDtypeStruct((nt, 1, LANES), _F32)],
        compiler_params=_cparams("arbitrary"),
        name="post",
    )(*ctx_ops, *lat_ops, mods_l, wts["w_out"], wts["norm2_g"], wts["router_w"], wts["router_b"])


MOE_TM = 512
_K_SHIFT = TOP_K.bit_length() - 1
assert 1 << _K_SHIFT == TOP_K


def _route(flat_e, cnt):
    a = flat_e.shape[0]
    t = a // TOP_K
    assert a & (a - 1) == 0 and t & (t - 1) == 0 and MOE_TM & (MOE_TM - 1) == 0
    nts = a // MOE_TM + N_EXPERTS
    tiles = (cnt + MOE_TM - 1) // MOE_TM
    tile_end = jnp.cumsum(tiles)
    n_tiles = tile_end[-1]
    tile_id = jnp.arange(nts, dtype=jnp.int32)
    tile_e = jnp.minimum(jnp.sum((tile_id[:, None] >= tile_end[None, :]).astype(jnp.int32), axis=1), N_EXPERTS - 1)
    tile_e = jnp.where(tile_id < n_tiles, tile_e, tile_e[jnp.maximum(n_tiles - 1, 0)])

    e_ids = jnp.arange(N_EXPERTS, dtype=jnp.int32)[:, None]
    j = jnp.arange(MOE_TM, dtype=jnp.int32)[None, :]
    n_pad = (tiles * MOE_TM - cnt)[:, None]
    pad_key = jnp.where(j < n_pad, e_ids * (2 * a) + a + j, N_EXPERTS * (2 * a) + e_ids * MOE_TM + j)
    key = jnp.sort(jnp.concatenate([flat_e * (2 * a) + jnp.arange(a, dtype=jnp.int32), pad_key.reshape(-1)]))
    idx = key & (2 * a - 1)
    assign = idx & (a - 1)
    real = (assign & (TOP_K - 1)) * t + lax.shift_right_logical(assign, _K_SHIFT)
    pos = jnp.arange(nts * MOE_TM, dtype=jnp.int32)
    spare = TOP_K * t + (lax.shift_right_logical(pos, MOE_TM.bit_length() - 1) & 1) * MOE_TM + (pos & (MOE_TM - 1))
    is_real = (idx < a) & (key < N_EXPERTS * (2 * a))
    code = jnp.concatenate([TOP_K * t + MOE_TM + j[0], jnp.where(is_real, real, spare)])
    return tile_e.astype(jnp.int32), n_tiles.reshape(1).astype(jnp.int32), code.astype(jnp.int32)


def _ffn_kernel(te_ref, nt_ref, code_ref, h2_hbm, w1_ref, b1_ref, w2_ref, b2_ref, yk_hbm,
                xb0, xb1, yb0, yb1, w1b, w2b, gsem, ssem, *, t):
    i = pl.program_id(0)
    nts = pl.num_programs(0)
    nt = nt_ref[0]
    xbuf = (xb0, xb1)
    ybuf = (yb0, yb1)
    rc = _ROW_CHUNKS
    spare0 = TOP_K * t * rc

    def tile_rows(ref, row):
        return ref.at[pl.ds(pl.multiple_of(row * rc, rc), rc)]

    def start_gather(tile, r, s, priority=0):
        tok = code_ref[(tile + 1) * MOE_TM + r] & (t - 1)
        pltpu.make_async_copy(tile_rows(h2_hbm, tok), tile_rows(xbuf[s], r), gsem.at[s]).start(priority=priority)

    def start_scatter(tile, r, s, priority=0):
        dst = code_ref[(tile + 1) * MOE_TM + r]
        pltpu.make_async_copy(tile_rows(ybuf[s], r), tile_rows(yk_hbm, dst), ssem.at[s]).start(priority=priority)

    def wait_gather(s):
        pltpu.make_async_copy(h2_hbm.at[pl.ds(0, MOE_TM * rc)], xbuf[s], gsem.at[s]).wait()

    def wait_scatter(s):
        pltpu.make_async_copy(ybuf[s], yk_hbm.at[pl.ds(0, MOE_TM * rc)], ssem.at[s]).wait()

    def loop_rows(fn):
        def body(r, carry):
            fn(r)
            return carry

        lax.fori_loop(0, MOE_TM, body, 0, unroll=8)

    @pl.when(i == 0)
    def _():
        yb0[...] = jnp.zeros_like(yb0)
        yb1[...] = jnp.zeros_like(yb1)
        pltpu.make_async_copy(yb0, yk_hbm.at[pl.ds(spare0, MOE_TM * rc)], ssem.at[0]).start()
        loop_rows(lambda r: start_gather(0, r, 0))

    def step(slot):
        other = 1 - slot
        x = jnp.concatenate([xbuf[slot][pl.ds(c, MOE_TM, stride=rc), :].astype(_BF) for c in range(rc)], axis=-1)
        nxt = jnp.minimum(i + 1, nts - 1)
        for r in range(MOE_TM):
            start_gather(nxt, r, other, priority=r % 2)
            start_scatter(i - 1, r, other, priority=(r + 1) % 2)
        gu = jnp.dot(x, w1b[...], preferred_element_type=_F32) + b1_ref[...]
        gt = jnp.minimum(gu[:, :D_FF], SWIGLU_LIMIT)
        up = jnp.clip(gu[:, D_FF:], -SWIGLU_LIMIT, SWIGLU_LIMIT)
        act = gt * jax.nn.sigmoid(SWIGLU_ALPHA * gt) * (up + 1.0)
        y = jnp.dot(act.astype(_BF), w2b[...], preferred_element_type=_F32) + b2_ref[...]
        wait_scatter(slot)
        for c in range(rc):
            ybuf[slot][pl.ds(c, MOE_TM, stride=rc), :] = y[:, c * LANES:(c + 1) * LANES]

    @pl.when(i < nt)
    def _():
        e_changed = jnp.logical_or(i == 0, te_ref[i] != te_ref[jnp.maximum(i - 1, 0)])

        @pl.when(e_changed)
        def _():
            w1b[...] = w1_ref[...].astype(_BF)
            w2b[...] = w2_ref[...].astype(_BF)

        for slot in range(2):
            @pl.when(i % 2 == slot)
            def _(slot=slot):
                wait_gather(slot)
                step(slot)

    @pl.when(i == nts - 1)
    def _():
        last = nt - 1
        for slot in range(2):
            @pl.when(last % 2 == slot)
            def _(slot=slot):
                loop_rows(lambda r: start_scatter(last, r, slot))
                wait_gather(1 - slot)

        wait_scatter(0)
        wait_scatter(1)


def _ffn(h2, tile_e, n_tiles, code, moe_w1, moe_w2, wts, layer):
    t = h2.shape[0] // _ROW_CHUNKS
    nts = tile_e.shape[0]
    by_expert = lambda i, te, nt, code: (te[i], 0, 0)
    by_layer_expert = lambda i, te, nt, code: (layer, te[i], 0, 0)
    grid_spec = pltpu.PrefetchScalarGridSpec(
        num_scalar_prefetch=3,
        grid=(nts,),
        in_specs=[
            pl.BlockSpec(memory_space=pl.ANY),
            pl.BlockSpec((None, None, D_MODEL, 2 * D_FF), by_layer_expert),
            pl.BlockSpec((None, 1, 2 * D_FF), by_expert),
            pl.BlockSpec((None, None, D_FF, D_MODEL), by_layer_expert),
            pl.BlockSpec((None, 1, D_MODEL), by_expert),
        ],
        out_specs=pl.BlockSpec(memory_space=pl.ANY),
        scratch_shapes=[
            pltpu.VMEM((MOE_TM * _ROW_CHUNKS, LANES), _F32),
            pltpu.VMEM((MOE_TM * _ROW_CHUNKS, LANES), _F32),
            pltpu.VMEM((MOE_TM * _ROW_CHUNKS, LANES), _F32),
            pltpu.VMEM((MOE_TM * _ROW_CHUNKS, LANES), _F32),
            pltpu.VMEM((D_MODEL, 2 * D_FF), _BF),
            pltpu.VMEM((D_FF, D_MODEL), _BF),
            pltpu.SemaphoreType.DMA((2,)),
            pltpu.SemaphoreType.DMA((2,)),
        ],
    )
    return pl.pallas_call(
        functools.partial(_ffn_kernel, t=t),
        grid_spec=grid_spec,
        out_shape=jax.ShapeDtypeStruct(((TOP_K * t + 2 * MOE_TM) * _ROW_CHUNKS, LANES), _F32),
        compiler_params=_cparams("arbitrary"),
        name="moe_ffn",
    )(tile_e, n_tiles, code, h2, moe_w1, wts["moe_b1"], moe_w2, wts["moe_b2"])


def _combine_kernel(x1c_ref, x1l_ref, tw_ref, mod_ref, y0_ref, y1_ref, y2_ref, y3_ref, oc_ref, ol_ref,
                    *, tm, ctx_tiles):
    y_refs = (y0_ref, y1_ref, y2_ref, y3_ref)

    def run(x1_ref, o_ref):
        tw = tw_ref[...]
        wk = [jnp.broadcast_to(tw[:, k:k + 1], (tm, LANES)) for k in range(TOP_K)]
        for c in range(_ROW_CHUNKS):
            cols = slice(c * LANES, (c + 1) * LANES)
            acc = wk[0] * y_refs[0][pl.ds(c, tm, stride=_ROW_CHUNKS), :]
            for k in range(1, TOP_K):
                acc = acc + wk[k] * y_refs[k][pl.ds(c, tm, stride=_ROW_CHUNKS), :]
            o_ref[:, cols] = x1_ref[:, cols] + mod_ref[5:6, cols] * acc

    i = pl.program_id(0)

    @pl.when(i < ctx_tiles)
    def _():
        run(x1c_ref, oc_ref)

    @pl.when(i >= ctx_tiles)
    def _():
        run(x1l_ref, ol_ref)


def _combine(x1c, x1l, tw, mods_l, yk, *, tm, seq):
    ctx_tiles = x1c.shape[0] // tm
    lat_tiles = x1l.shape[0] // tm
    nt = ctx_tiles + lat_tiles
    per_seq = seq // tm
    ctx = lambda i: (jnp.minimum(i, ctx_tiles - 1), 0)
    lat = lambda i: (jnp.maximum(i - ctx_tiles, 0), 0)
    mod_idx = lambda i: (jnp.where(i < ctx_tiles, 0, 1 + (i - ctx_tiles) // per_seq), 0, 0)
    plane = lambda k: (lambda i: (k * nt + i, 0))
    return pl.pallas_call(
        functools.partial(_combine_kernel, tm=tm, ctx_tiles=ctx_tiles),
        grid=(nt,),
        in_specs=[pl.BlockSpec((tm, D_MODEL), ctx), pl.BlockSpec((tm, D_MODEL), lat),
                  pl.BlockSpec((tm, LANES), lambda i: (i, 0)),
                  pl.BlockSpec((None, 6, D_MODEL), mod_idx)]
                 + [pl.BlockSpec((tm * _ROW_CHUNKS, LANES), plane(k)) for k in range(TOP_K)],
        out_specs=[pl.BlockSpec((tm, D_MODEL), ctx), pl.BlockSpec((tm, D_MODEL), lat)],
        out_shape=[jax.ShapeDtypeStruct(x1c.shape, _F32), jax.ShapeDtypeStruct(x1l.shape, _F32)],
        compiler_params=_cparams("arbitrary"),
        name="moe_combine",
    )(x1c, x1l, tw, mods_l, yk, yk, yk, yk)


def _pad_heads(w, heads, dim):
    lead = w.shape[:-1]
    w = w.reshape(lead + (heads, dim))
    w = jnp.pad(w, [(0, 0)] * len(lead) + [(0, 0), (0, LANES - dim)])
    return w.reshape(lead + (heads * LANES,))


def _pad_lanes(g, left=0):
    return jnp.pad(g, (left, LANES - left - g.shape[0])).reshape(1, LANES)


def _block_diag_halves(w):
    w4 = w.reshape(2, 4, 64, 64)
    eye = jnp.eye(4, dtype=w.dtype)
    return jnp.einsum("jaik,ab->jaibk", w4, eye).reshape(2, 256, 256)


def _layer_weights(l, p):
    w_in = p["w_in"][l]
    qa, ka, va, xb, yb, cq, ckv, kr = jnp.split(w_in, (512, 640, 768, 1280, 1792, 2176, 2432), axis=-1)
    kr128 = jnp.pad(kr, ((0, 0), (MLA_NOPE, LANES - MLA_NOPE - MLA_ROPE)))
    w_in_p = jnp.concatenate([_pad_heads(qa, A_HEADS, A_HEAD_DIM), _pad_heads(ka, A_KV_HEADS, A_HEAD_DIM), va, xb,
                              yb, cq, ckv, kr128], axis=-1).astype(_BF)
    ukv = p["mla_w_ukv"][l].reshape(KV_LORA, MLA_HEADS, MLA_NOPE + MLA_V)
    w_ukv = jnp.concatenate([_pad_heads(ukv[:, :, :MLA_NOPE].reshape(KV_LORA, -1), MLA_HEADS, MLA_NOPE),
                             ukv[:, :, MLA_NOPE:].reshape(KV_LORA, -1)], axis=-1).astype(_BF)
    wg = jnp.concatenate([_block_diag_halves(p["lru_wa"][l, 0]), _block_diag_halves(p["lru_wx"][l, 0]),
                          _block_diag_halves(p["lru_wa"][l, 1]), _block_diag_halves(p["lru_wx"][l, 1])],
                         axis=-1).astype(_BF)
    bg = jnp.stack([p["lru_ba"][l, 0], p["lru_bx"][l, 0], p["lru_ba"][l, 1], p["lru_bx"][l, 1]])
    return {
        "norm1_g": p["norm1_g"][l].reshape(1, -1),
        "norm2_g": p["norm2_g"][l].reshape(1, -1),
        "w_in": w_in_p,
        "a_q_g": _pad_lanes(p["a_q_g"][l]),
        "a_k_g": _pad_lanes(p["a_k_g"][l]),
        "mla_cq_g": p["mla_cq_g"][l].reshape(1, -1),
        "w_uq": _pad_heads(p["mla_w_uq"][l], MLA_HEADS, MLA_QK).astype(_BF),
        "mla_q_g": _pad_lanes(p["mla_q_g"][l]),
        "mla_ckv_g": p["mla_ckv_g"][l].reshape(1, -1),
        "w_ukv": w_ukv,
        "mla_k_g": _pad_lanes(p["mla_k_g"][l]),
        "conv_w": p["conv_w"][l],
        "conv_b": p["conv_b"][l].reshape(1, -1),
        "lru_wg": wg,
        "lru_bg": bg,
        "lru_lam": p["lru_lam"][l],
        "w_out": p["w_out"][l].astype(_BF),
        "router_w": jnp.pad(p["router_w"][l], ((0, 0), (0, LANES - N_EXPERTS))),
        "router_b": jnp.pad(p["router_b"][l], (0, LANES - N_EXPERTS), constant_values=-1e30).reshape(1, LANES),
        "moe_b1": p["moe_b1"][l].reshape(N_EXPERTS, 1, 2 * D_FF),
        "moe_b2": p["moe_b2"][l].reshape(N_EXPERTS, 1, D_MODEL),
    }


def _rope_tables(n, rot_dim, lane0):
    rows = n // GRID_W
    r = jnp.repeat(jnp.arange(rows, dtype=_F32), GRID_W)
    col = jnp.tile(jnp.arange(GRID_W, dtype=_F32), rows)
    n_freq = rot_dim // 4
    inv = ROPE_THETA ** (-jnp.arange(n_freq, dtype=_F32) / n_freq)
    ang = jnp.concatenate([r[:, None] * inv, col[:, None] * inv], axis=-1)
    cos = jnp.repeat(jnp.cos(ang), 2, axis=-1)
    sin = jnp.repeat(jnp.sin(ang), 2, axis=-1) * jnp.tile(jnp.array([-1.0, 1.0], _F32), rot_dim // 2)
    pad = ((0, 0), (lane0, LANES - lane0 - rot_dim))
    return jnp.pad(cos, pad, constant_values=1.0), jnp.pad(sin, pad)


def kernel(x_prompt, x_sample, cache_attn_k, cache_attn_v, state_lru, cache_mla_ckv, cache_mla_krope, c, c_ctx,
           ada_w, ada_b, norm1_g, norm2_g, w_in, a_q_g, a_k_g, conv_w, conv_b, lru_wa, lru_ba, lru_wx, lru_bx,
           lru_lam, mla_cq_g, mla_w_uq, mla_q_g, mla_ckv_g, mla_w_ukv, mla_k_g, w_out, router_w, router_b,
           moe_w1, moe_b1, moe_w2, moe_b2):
    p = dict(norm1_g=norm1_g, norm2_g=norm2_g, w_in=w_in, a_q_g=a_q_g, a_k_g=a_k_g, conv_w=conv_w, conv_b=conv_b,
             lru_wa=lru_wa, lru_ba=lru_ba, lru_wx=lru_wx, lru_bx=lru_bx, lru_lam=lru_lam, mla_cq_g=mla_cq_g,
             mla_w_uq=mla_w_uq, mla_q_g=mla_q_g, mla_ckv_g=mla_ckv_g, mla_w_ukv=mla_w_ukv, mla_k_g=mla_k_g,
             w_out=w_out, router_w=router_w, router_b=router_b, moe_w1=moe_w1, moe_b1=moe_b1, moe_w2=moe_w2,
             moe_b2=moe_b2)
    batch, seq, _ = x_prompt.shape
    dec_batch, dec_seq, _ = x_sample.shape
    past = cache_attn_k.shape[2]

    cvec = jnp.concatenate([c_ctx[None, :], c, jnp.zeros((N_MODS - 1 - dec_batch, D_MODEL), _F32)], axis=0)
    mods = _ada_mods(cvec, ada_w, ada_b)
    cos_a, sin_a = _rope_tables(dec_seq, A_HEAD_DIM, 0)
    cos_c, sin_c = _rope_tables(dec_seq, MLA_ROPE, MLA_NOPE)
    tabs = (cos_a, sin_a, cos_c, sin_c)

    yp = x_prompt.reshape(batch * seq, D_MODEL)
    ys = x_sample.reshape(dec_batch * dec_seq, D_MODEL)
    zero_state = jnp.zeros((batch, 2, LRU_WIDTH), _F32)
    ks_l, vs_l, hs_l, ckv_l, kr_l = [], [], [], [], []
    tm = 512
    tm_rope = 256
    tm_combine = 256
    tq = 512
    n_tok = batch * seq + dec_batch * dec_seq
    for l in range(DEPTH):
        wts = _layer_weights(l, p)
        mods_l = mods[l]

        qa, kab, kaf, vab, vaf, xb, yb, qc, ckvn, kr128, kr = _proj(yp, mods_l, wts, None, rope=False, tm=tm, seq=seq)
        kc, vc = _mlakv(ckvn, kr128, wts, None, rope=False, tm=tm, seq=seq)
        ob, st = _lru(xb, yb, zero_state, wts, n=seq)
        oa = _attn(qa, kab, vab, None, None, heads=A_HEADS, kv_heads=A_KV_HEADS, dv=A_HEAD_DIM, n=seq, tq=seq,
                   name="attn_a_ctx")
        oc = _attn(qc, kc, vc, None, None, heads=MLA_HEADS, kv_heads=MLA_HEADS, dv=MLA_V, n=seq, tq=seq,
                   name="attn_c_ctx")
        ctx_ops = (oa, ob, oc, yp)
        ks_l.append(kaf.reshape(batch, seq, A_KV_HEADS, A_HEAD_DIM))
        vs_l.append(vaf.reshape(batch, seq, A_KV_HEADS, A_HEAD_DIM))
        hs_l.append(st)
        ckv_l.append(ckvn.reshape(batch, seq, KV_LORA))
        kr_l.append(kr.reshape(batch, seq, MLA_ROPE))

        qa, kab, _, vab, _, xb, yb, qc, ckvn, kr128, _ = _proj(ys, mods_l, wts, tabs, rope=True, tm=tm_rope,
                                                               seq=dec_seq)
        kc, vc = _mlakv(ckvn, kr128, wts, tabs, rope=True, tm=tm, seq=dec_seq)
        c_kr128 = jnp.pad(cache_mla_krope[:, l].reshape(dec_batch * past, MLA_ROPE),
                          ((0, 0), (MLA_NOPE, LANES - MLA_NOPE - MLA_ROPE)))
        kx, vx = _mlakv(cache_mla_ckv[:, l].reshape(dec_batch * past, KV_LORA), c_kr128, wts, None, rope=False,
                        tm=tm, seq=past)
        ob, _ = _lru(xb, yb, state_lru[:, l], wts, n=dec_seq)
        c_k = _pad_heads(cache_attn_k[:, l].reshape(dec_batch * past, A_KV_HEADS * A_HEAD_DIM), A_KV_HEADS,
                         A_HEAD_DIM).astype(_BF)
        c_v = cache_attn_v[:, l].reshape(dec_batch * past, A_KV_HEADS * A_HEAD_DIM).astype(_BF)
        oa = _attn(qa, kab, vab, c_k, c_v, heads=A_HEADS, kv_heads=A_KV_HEADS, dv=A_HEAD_DIM, n=dec_seq, tq=tq,
                   name="attn_a_lat")
        oc = _attn(qc, kc, vc, kx, vx, heads=MLA_HEADS, kv_heads=MLA_HEADS, dv=MLA_V, n=dec_seq, tq=tq,
                   name="attn_c_lat")
        x1c, x1l, h2, tidx, tw, hist = _post(ctx_ops, (oa, ob, oc, ys), mods_l, wts, tm=tm, seq=dec_seq)
        flat_e = tidx[:, :TOP_K].reshape(n_tok * TOP_K)
        cnt = jnp.sum(hist, axis=(0, 1))[:N_EXPERTS].astype(jnp.int32)
        tile_e, n_tiles, code = _route(flat_e, cnt)
        yk = _ffn(h2, tile_e, n_tiles, code, moe_w1, moe_w2, wts, l)
        yp, ys = _combine(x1c, x1l, tw, mods_l, yk, tm=tm_combine, seq=dec_seq)

    return (yp.reshape(batch, seq, D_MODEL), ys.reshape(dec_batch, dec_seq, D_MODEL),
            jnp.stack(ks_l, axis=1), jnp.stack(vs_l, axis=1), jnp.stack(hs_l, axis=1),
            jnp.stack(ckv_l, axis=1), jnp.stack(kr_l, axis=1))
```
